```python
import math
import jax, jax.numpy as jnp
from jax import lax
import numpy as np

D_MODEL = 1024
BATCH = 8
SEQ = 4096
DEPTH = 1

D_MIX = D_MODEL
D_CONV = D_MIX // 2
D_ATTN = D_MIX - D_CONV
HEAD_DIM = 64
N_HEADS = D_ATTN // HEAD_DIM
N_CONV_GROUPS = D_CONV // HEAD_DIM
CONV_WIDTH = 31
BLOCK = 256
TOP_K = 3
Q_CHUNK = 32
ROPE_DIM = HEAD_DIM // 4
ROPE_THETA = 500000.0
EPS = 1e-6
NEG = -1e30
SPLITS = (D_CONV, D_CONV, D_CONV, D_ATTN, D_ATTN, D_ATTN, D_ATTN)
D_IN = sum(SPLITS)

kernel_name = "hybrid_conformer_conv_moba_parallel_heads"


def rmsnorm(x, g):
    xf = x.astype(jnp.float32)
    y = xf * lax.rsqrt(jnp.mean(xf * xf, axis=-1, keepdims=True) + EPS)
    return (y * g.astype(jnp.float32)).astype(x.dtype)


def layernorm(x, g, b):
    xf = x.astype(jnp.float32)
    mu = jnp.mean(xf, axis=-1, keepdims=True)
    var = jnp.mean(jnp.square(xf - mu), axis=-1, keepdims=True)
    y = (xf - mu) * lax.rsqrt(var + EPS)
    return (y * g.astype(jnp.float32) + b.astype(jnp.float32)).astype(x.dtype)


def partial_rope(t, positions):
    half = ROPE_DIM // 2
    inv = ROPE_THETA ** (-(jnp.arange(half, dtype=jnp.float32) * 2.0) / ROPE_DIM)
    ang = positions.astype(jnp.float32)[..., None] * inv
    cos = jnp.cos(ang)[:, :, None, :]
    sin = jnp.sin(ang)[:, :, None, :]
    t1 = t[..., :half].astype(jnp.float32)
    t2 = t[..., half:ROPE_DIM].astype(jnp.float32)
    r1 = (t1 * cos - t2 * sin).astype(t.dtype)
    r2 = (t2 * cos + t1 * sin).astype(t.dtype)
    return jnp.concatenate([r1, r2, t[..., ROPE_DIM:]], axis=-1)


def conformer_conv(val, glu_gate, w_dw, b_dw, g_ln, b_ln, w_pw, b_pw):
    u = val * jax.nn.sigmoid(glu_gate)
    u = lax.conv_general_dilated(
        u, w_dw.reshape(CONV_WIDTH, 1, D_CONV).astype(u.dtype),
        window_strides=(1,), padding=[(CONV_WIDTH - 1, 0)],
        dimension_numbers=("NWC", "WIO", "NWC"),
        feature_group_count=D_CONV) + b_dw
    u = jax.nn.silu(layernorm(u, g_ln, b_ln))
    return u @ w_pw + b_pw


def moba_attention(q, k, v):
    B, S, H, Dh = q.shape
    nb = -(-S // BLOCK)
    pad = nb * BLOCK - S
    q = q.transpose(0, 2, 1, 3)
    kp = jnp.pad(k.transpose(0, 2, 1, 3), ((0, 0), (0, 0), (0, pad), (0, 0)))
    vp = jnp.pad(v.transpose(0, 2, 1, 3), ((0, 0), (0, 0), (0, pad), (0, 0)))
    kb = kp.reshape(B, H, nb, BLOCK, Dh)
    vb = vp.reshape(B, H, nb, BLOCK, Dh)
    scale = Dh ** -0.5
    q_block = jnp.arange(S) // BLOCK
    n_sel = min(TOP_K, nb - 1)
    if n_sel > 0:
        kmean = jnp.mean(kb.astype(jnp.float32), axis=3)
        gate = jnp.einsum("bhsd,bhnd->bhsn", q.astype(jnp.float32), kmean)
        past = jnp.arange(nb)[None, :] < q_block[:, None]
        gate = jnp.where(past[None, None], gate, NEG)
        _, sel = lax.top_k(gate, n_sel)
        valid = jnp.arange(n_sel)[None, :] < q_block[:, None]
    bidx = jnp.arange(B)[:, None, None]
    hidx = jnp.arange(H)[None, :, None]

    def chunk(i):
        start = i * Q_CHUNK
        qc = lax.dynamic_slice_in_dim(q, start, Q_CHUNK, axis=2)
        tpos = start + jnp.arange(Q_CHUNK)
        own = start // BLOCK
        ko = lax.dynamic_slice_in_dim(kp, own * BLOCK, BLOCK, axis=2)
        vo = lax.dynamic_slice_in_dim(vp, own * BLOCK, BLOCK, axis=2)
        kpos = own * BLOCK + jnp.arange(BLOCK)
        lo = jnp.einsum("bhqd,bhkd->bhqk", qc, ko).astype(jnp.float32) * scale
        lo = jnp.where((kpos[None, :] <= tpos[:, None])[None, None], lo, NEG)
        logits = [lo]
        if n_sel > 0:
            selc = lax.dynamic_slice_in_dim(sel, start, Q_CHUNK, axis=2)
            validc = lax.dynamic_slice_in_dim(valid, start, Q_CHUNK, axis=0)
            for r in range(n_sel):
                kg = kb[bidx, hidx, selc[..., r]]
                lr = jnp.einsum("bhqd,bhqkd->bhqk", qc, kg).astype(jnp.float32) * scale
                logits.append(jnp.where(validc[None, None, :, r:r + 1], lr, NEG))
        probs = jax.nn.softmax(jnp.concatenate(logits, axis=-1), axis=-1).astype(vp.dtype)
        out = jnp.einsum("bhqk,bhkd->bhqd", probs[..., :BLOCK], vo)
        if n_sel > 0:
            for r in range(n_sel):
                vg = vb[bidx, hidx, selc[..., r]]
                pr = probs[..., (r + 1) * BLOCK:(r + 2) * BLOCK]
                out = out + jnp.einsum("bhqk,bhqkd->bhqd", pr, vg)
        return out

    outs = lax.map(chunk, jnp.arange(S // Q_CHUNK))
    return outs.transpose(1, 0, 3, 2, 4).reshape(B, S, H * Dh)


def setup_inputs(seed: int = 0) -> dict:
    key = jax.random.key(seed)
    ks = jax.random.split(key, 16)
    f32 = jnp.float32
    x = jax.random.normal(ks[0], (BATCH, SEQ, D_MODEL), f32)
    c = jax.random.normal(ks[1], (BATCH, D_MODEL), f32)
    positions = jnp.broadcast_to(jnp.arange(SEQ, dtype=jnp.int32), (BATCH, SEQ))
    w_ada = jax.random.normal(ks[2], (DEPTH, D_MODEL, 3 * D_MODEL), f32) * (0.5 * D_MODEL ** -0.5)
    b_ada = jax.random.normal(ks[3], (DEPTH, 3 * D_MODEL), f32) * 0.02
    g_norm = 1.0 + 0.02 * jax.random.normal(ks[4], (DEPTH, D_MODEL), f32)
    w_in = jax.random.normal(ks[5], (DEPTH, D_MODEL, D_IN), f32) * D_MODEL ** -0.5
    w_dw = jax.random.normal(ks[6], (DEPTH, CONV_WIDTH, D_CONV), f32) * CONV_WIDTH ** -0.5
    b_dw = jax.random.normal(ks[7], (DEPTH, D_CONV), f32) * 0.02
    g_ln_conv = 1.0 + 0.02 * jax.random.normal(ks[8], (DEPTH, D_CONV), f32)
    b_ln_conv = jax.random.normal(ks[9], (DEPTH, D_CONV), f32) * 0.02
    w_pw = jax.random.normal(ks[10], (DEPTH, D_CONV, D_CONV), f32) * D_CONV ** -0.5
    b_pw = jax.random.normal(ks[11], (DEPTH, D_CONV), f32) * 0.02
    w_out = jax.random.normal(ks[12], (DEPTH, D_MIX, D_MODEL), f32) * D_MIX ** -0.5
    g_final = 1.0 + 0.02 * jax.random.normal(ks[13], (D_MODEL,), f32)
    return {"x": x, "c": c, "positions": positions, "w_ada": w_ada, "b_ada": b_ada,
            "g_norm": g_norm, "w_in": w_in, "w_dw": w_dw, "b_dw": b_dw,
            "g_ln_conv": g_ln_conv, "b_ln_conv": b_ln_conv, "w_pw": w_pw, "b_pw": b_pw,
            "w_out": w_out, "g_final": g_final}


def reference(x, c, positions, w_ada, b_ada, g_norm, w_in, w_dw, b_dw, g_ln_conv,
              b_ln_conv, w_pw, b_pw, w_out, g_final):
    B, S, _ = x.shape
    cut = np.cumsum(SPLITS)[:-1].tolist()
    for l in range(DEPTH):
        mod = jax.nn.silu(c) @ w_ada[l] + b_ada[l]
        shift, scale, gate = jnp.split(mod, 3, axis=-1)
        h = rmsnorm(x, g_norm[l]) * (1.0 + scale[:, None, :]) + shift[:, None, :]
        proj = h @ w_in[l]
        c_val, c_glu, c_gate, q, k, v, a_gate = jnp.split(proj, cut, axis=-1)
        y_conv = conformer_conv(c_val, c_glu, w_dw[l], b_dw[l], g_ln_conv[l],
                                b_ln_conv[l], w_pw[l], b_pw[l]) * jax.nn.silu(c_gate)
        q = partial_rope(q.reshape(B, S, N_HEADS, HEAD_DIM), positions)
        k = partial_rope(k.reshape(B, S, N_HEADS, HEAD_DIM), positions)
        v = v.reshape(B, S, N_HEADS, HEAD_DIM)
        y_attn = moba_attention(q, k, v) * jax.nn.silu(a_gate)
        y = jnp.concatenate([y_conv, y_attn], axis=-1) @ w_out[l]
        x = x + gate[:, None, :] * y
    return rmsnorm(x, g_final)
```

```python
import functools

import jax
import jax.numpy as jnp
import numpy as np
from jax import lax
from jax.experimental import pallas as pl
from jax.experimental.pallas import tpu as pltpu

F32 = jnp.float32
BF16 = jnp.bfloat16

D_MODEL = 1024
D_CONV = 512
D_ATTN = 512
HEAD_DIM = 64
N_HEADS = 8
CONV_WIDTH = 31
BLOCK = 256
TOP_K = 3
ROPE_DIM = 16
ROPE_HALF = 8
ROPE_THETA = 500000.0
EPS = 1e-6
NEG = -1e30
D_IN = 3584
C_VAL, C_GLU, C_GATE, C_Q, C_K, C_V, C_AG = 0, 512, 1024, 1536, 2048, 2560, 3072

LANES = 128
HALO = 32
CONV_TILE = 256
CONV_CHUNK = 32
OUT_TILE = 512
VMEM_LIMIT = 48 * 1024 * 1024

_NT = (((1,), (1,)), ((), ()))


def _split_bf16(a):
    hi = a.astype(BF16)
    lo = (a - hi.astype(F32)).astype(BF16)
    return hi, lo


def _sigmoid(a):
    return 1.0 / (1.0 + jnp.exp(-a))


def _silu(a):
    return a * _sigmoid(a)


def _adaln_body(c_ref, w_ref, b_ref, o_ref):
    a = _silu(c_ref[...])
    a_hi, a_lo = _split_bf16(a)
    w_hi, w_lo = _split_bf16(w_ref[...])
    acc = jnp.dot(a_hi, w_hi, preferred_element_type=F32)
    acc = acc + (jnp.dot(a_hi, w_lo, preferred_element_type=F32)
                 + jnp.dot(a_lo, w_hi, preferred_element_type=F32))
    o_ref[...] = acc + b_ref[...]


def _adaln(c, w_ada, b_ada):
    bsz = c.shape[0]
    n_out = w_ada.shape[1]
    tn = 1024
    return pl.pallas_call(
        _adaln_body,
        grid=(n_out // tn,),
        in_specs=[pl.BlockSpec((bsz, D_MODEL), lambda j: (0, 0)),
                  pl.BlockSpec((D_MODEL, tn), lambda j: (0, j)),
                  pl.BlockSpec((1, tn), lambda j: (0, j))],
        out_specs=pl.BlockSpec((bsz, tn), lambda j: (0, j)),
        out_shape=jax.ShapeDtypeStruct((bsz, n_out), F32),
        compiler_params=pltpu.CompilerParams(dimension_semantics=("arbitrary",),
                                             vmem_limit_bytes=VMEM_LIMIT),
        name="adaln",
    )(c, w_ada, b_ada.reshape(1, n_out))


def _rope_table_body(inv_ref, pos_ref, cos_ref, sin_ref):
    pos = pos_ref[...].astype(F32)
    for i in range(ROPE_HALF):
        ang = pos * inv_ref[i]
        cos_ref[i] = jnp.cos(ang)
        sin_ref[i] = jnp.sin(ang)


def _rope_tables(positions):
    bsz, seq = positions.shape
    inv = ROPE_THETA ** (-(jnp.arange(ROPE_HALF, dtype=F32) * 2.0) / ROPE_DIM)
    cos, sin = pl.pallas_call(
        _rope_table_body,
        in_specs=[pl.BlockSpec(memory_space=pltpu.SMEM),
                  pl.BlockSpec((bsz, seq), lambda: (0, 0))],
        out_specs=[pl.BlockSpec((ROPE_HALF, bsz, seq), lambda: (0, 0, 0))] * 2,
        out_shape=[jax.ShapeDtypeStruct((ROPE_HALF, bsz, seq), F32)] * 2,
        name="rope_table",
    )(inv, positions)
    cos = cos.transpose(1, 2, 0).reshape(bsz * seq, ROPE_HALF)
    sin = sin.transpose(1, 2, 0).reshape(bsz * seq, ROPE_HALF)
    ones = jnp.ones((bsz * seq, HEAD_DIM - ROPE_DIM), F32)
    cpat = jnp.concatenate([cos, cos, ones] * 2, axis=1)
    spat = jnp.concatenate([-sin, sin, 0.0 * ones] * 2, axis=1)
    return cpat, spat


def _inproj_body(x_ref, mod_ref, g_ref, cp_ref, sp_ref, whi_ref, wlo_ref,
                 u_ref, sgc_ref, q_ref, k_ref, v_ref, sga_ref, bias_ref, km_ref):
    i = pl.program_id(1)

    @pl.when(i == 0)
    def _():
        km_ref[...] = jnp.zeros_like(km_ref)

    x = x_ref[0]
    r = lax.rsqrt(jnp.mean(x * x, axis=-1, keepdims=True) + EPS)
    shift = mod_ref[0, 0:1, :]
    scale = mod_ref[0, 1:2, :]
    h = (x * r) * (g_ref[...] * (1.0 + scale)) + shift
    h_hi, h_lo = _split_bf16(h)

    def proj(c0):
        return jnp.dot(h_hi, whi_ref[:, c0:c0 + 512], preferred_element_type=F32)

    def proj3(c0, l0):
        w_h = whi_ref[:, c0:c0 + 512]
        w_l = wlo_ref[:, l0:l0 + 512]
        return (jnp.dot(h_hi, w_h, preferred_element_type=F32)
                + (jnp.dot(h_hi, w_l, preferred_element_type=F32)
                   + jnp.dot(h_lo, w_h, preferred_element_type=F32)))

    u_ref[...] = (proj(C_VAL) * _sigmoid(proj(C_GLU))).astype(BF16)
    sgc_ref[...] = _silu(proj(C_GATE)).astype(BF16)
    v_ref[...] = proj(C_V).astype(BF16)
    sga_ref[...] = _silu(proj(C_AG)).astype(BF16)

    cp = cp_ref[...]
    sp = sp_ref[...]
    lane = lax.broadcasted_iota(jnp.int32, (BLOCK, LANES), 1)
    first_half = (lane % HEAD_DIM) < ROPE_HALF

    def rope(t):
        outs = []
        for g in range(D_ATTN // LANES):
            tg = t[:, g * LANES:(g + 1) * LANES]
            rot = jnp.where(first_half, pltpu.roll(tg, LANES - ROPE_HALF, 1), pltpu.roll(tg, ROPE_HALF, 1))
            outs.append(tg * cp + rot * sp)
        return jnp.concatenate(outs, axis=1)

    qs = rope(proj3(C_Q, 0)) * (HEAD_DIM ** -0.5)
    kr = rope(proj3(C_K, 512))
    q_hi, q_lo = _split_bf16(qs)
    q_ref[...] = q_hi
    k_ref[...] = kr.astype(BF16)

    km = km_ref[...]
    lane_k = lax.broadcasted_iota(jnp.int32, km.shape, 1)
    kmt = jnp.concatenate([jnp.where((lane_k // HEAD_DIM) == hd, km, 0.0) for hd in range(N_HEADS)], axis=0)
    kmt_hi, kmt_lo = _split_bf16(kmt)
    gt = (lax.dot_general(kmt_hi, q_hi, _NT, preferred_element_type=F32)
          + (lax.dot_general(kmt_hi, q_lo, _NT, preferred_element_type=F32)
             + lax.dot_general(kmt_lo, q_hi, _NT, preferred_element_type=F32)))

    nb = km.shape[0]
    n_iota = lax.broadcasted_iota(jnp.int32, (nb, BLOCK), 0)
    past = n_iota < i
    n_sel = jnp.minimum(i, TOP_K)
    biases = []
    for hd in range(N_HEADS):
        g = jnp.where(past, gt[hd * nb:(hd + 1) * nb, :], NEG)
        rank = jnp.zeros((nb, BLOCK), jnp.int32)
        for n2 in range(nb):
            row = g[n2:n2 + 1, :]
            beats = (row > g) | ((row == g) & (n_iota > n2))
            rank = rank + jnp.where(beats, 1, 0)
        keep = (rank < n_sel) | (n_iota == i)
        biases.append(jnp.where(keep, 0.0, NEG))
    bias_t = jnp.concatenate(biases, axis=0)
    bias_ref[...] = bias_t.T.astype(BF16)

    km_ref[pl.ds(i, 1), :] = jnp.mean(kr, axis=0, keepdims=True)


def _inproj(x, mod3, g_norm, cpat, spat, w_hi, w_lo):
    bsz, seq, _ = x.shape
    nblk = seq // BLOCK
    n = bsz * seq
    row = lambda b, i: (b * nblk + i, 0)
    const = lambda b, i: (0, 0)
    out_sd = lambda w: jax.ShapeDtypeStruct((n, w), BF16)
    return pl.pallas_call(
        _inproj_body,
        grid=(bsz, nblk),
        in_specs=[pl.BlockSpec((1, BLOCK, D_MODEL), lambda b, i: (b, i, 0)),
                  pl.BlockSpec((1, 3, D_MODEL), lambda b, i: (b, 0, 0)),
                  pl.BlockSpec((1, D_MODEL), const),
                  pl.BlockSpec((BLOCK, LANES), row),
                  pl.BlockSpec((BLOCK, LANES), row),
                  pl.BlockSpec((D_MODEL, D_IN), const),
                  pl.BlockSpec((D_MODEL, 2 * D_ATTN), const)],
        out_specs=[pl.BlockSpec((BLOCK, 512), row)] * 6 + [pl.BlockSpec((BLOCK, LANES), row)],
        out_shape=[out_sd(512)] * 6 + [out_sd(LANES)],
        scratch_shapes=[pltpu.VMEM((nblk, D_ATTN), F32)],
        compiler_params=pltpu.CompilerParams(dimension_semantics=("arbitrary", "arbitrary"),
                                             vmem_limit_bytes=VMEM_LIMIT),
        name="inproj",
    )(x, mod3, g_norm, cpat, spat, w_hi, w_lo)


def _conv_body(u_ref, halo_ref, sgc_ref, wdw_ref, bdw_ref, gln_ref, bln_ref, wpw_ref, bpw_ref,
               o_ref, win_ref, conv_ref):
    t = pl.program_id(1)
    halo = halo_ref[...].astype(F32)
    win_ref[0:HALO, :] = jnp.where(t == 0, 0.0, halo)
    win_ref[HALO:, :] = u_ref[...].astype(F32)
    base = HALO - (CONV_WIDTH - 1)
    for c in range(CONV_TILE // CONV_CHUNK):
        r0 = c * CONV_CHUNK
        acc = jnp.zeros((CONV_CHUNK, D_CONV), F32)
        for w in range(CONV_WIDTH):
            acc = acc + win_ref[r0 + base + w:r0 + base + w + CONV_CHUNK, :] * wdw_ref[w:w + 1, :]
        conv_ref[r0:r0 + CONV_CHUNK, :] = acc
    y = conv_ref[...] + bdw_ref[...]
    mu = jnp.mean(y, axis=-1, keepdims=True)
    yc = y - mu
    var = jnp.mean(yc * yc, axis=-1, keepdims=True)
    z = _silu(yc * lax.rsqrt(var + EPS) * gln_ref[...] + bln_ref[...])
    pw = jnp.dot(z.astype(BF16), wpw_ref[...], preferred_element_type=F32) + bpw_ref[...]
    o_ref[...] = (pw * sgc_ref[...].astype(F32)).astype(BF16)


def _conv(u, sgc, w_dw, b_dw, g_ln, b_ln, w_pw_bf, b_pw, bsz, seq):
    n = bsz * seq
    nt = seq // CONV_TILE
    row = lambda b, t: (b * nt + t, 0)
    const = lambda b, t: (0, 0)
    halo_idx = lambda b, t: (jnp.maximum((b * seq + t * CONV_TILE) // HALO - 1, 0), 0)
    vec = pl.BlockSpec((1, D_CONV), const)
    return pl.pallas_call(
        _conv_body,
        grid=(bsz, nt),
        in_specs=[pl.BlockSpec((CONV_TILE, D_CONV), row),
                  pl.BlockSpec((HALO, D_CONV), halo_idx),
                  pl.BlockSpec((CONV_TILE, D_CONV), row),
                  pl.BlockSpec((CONV_WIDTH, D_CONV), const),
                  vec, vec, vec,
                  pl.BlockSpec((D_CONV, D_CONV), const),
                  vec],
        out_specs=pl.BlockSpec((CONV_TILE, D_CONV), row),
        out_shape=jax.ShapeDtypeStruct((n, D_CONV), BF16),
        scratch_shapes=[pltpu.VMEM((CONV_TILE + HALO, D_CONV), F32),
                        pltpu.VMEM((CONV_TILE, D_CONV), F32)],
        compiler_params=pltpu.CompilerParams(dimension_semantics=("arbitrary", "arbitrary"),
                                             vmem_limit_bytes=VMEM_LIMIT),
        name="conv",
    )(u, u, sgc, w_dw, b_dw.reshape(1, -1), g_ln.reshape(1, -1), b_ln.reshape(1, -1), w_pw_bf,
      b_pw.reshape(1, -1))


def _attn_body(q_ref, bias_ref, k_ref, v_ref, sga_ref, o_ref, acc_ref):
    p = pl.program_id(1)
    i = pl.program_id(2)
    lane = lax.broadcasted_iota(jnp.int32, (BLOCK, LANES), 1)
    q = q_ref[...].astype(F32)
    bias = bias_ref[...].astype(F32)
    qa = []
    for hh in range(2):
        qm = jnp.where((lane // HEAD_DIM) == hh, q, 0.0)
        bm = jnp.where((lane // (LANES // N_HEADS)) == 2 * p + hh, bias, 0.0)
        qa.append(jnp.concatenate([qm, bm], axis=1).astype(BF16))

    def scores(j, hh):
        kj = k_ref[pl.ds(pl.multiple_of(j * BLOCK, BLOCK), BLOCK), :]
        onehot = jnp.where((lane % (LANES // N_HEADS)) == j, 1.0, 0.0).astype(BF16)
        ka = jnp.concatenate([kj, onehot], axis=1)
        return lax.dot_general(qa[hh], ka, _NT, preferred_element_type=F32)

    def update(j, hh, s, m, l):
        vj = v_ref[pl.ds(pl.multiple_of(j * BLOCK, BLOCK), BLOCK), :]
        m_new = jnp.maximum(m, jnp.max(s, axis=-1, keepdims=True))
        pr = jnp.exp(s - m_new)
        alpha = jnp.exp(m - m_new)
        l_new = alpha * l + jnp.sum(pr, axis=-1, keepdims=True)
        acc_ref[hh] = alpha * acc_ref[hh] + jnp.dot(pr.astype(BF16), vj, preferred_element_type=F32)
        return m_new, l_new

    rows = lax.broadcasted_iota(jnp.int32, (BLOCK, BLOCK), 0)
    cols = lax.broadcasted_iota(jnp.int32, (BLOCK, BLOCK), 1)
    causal = cols <= rows
    acc_ref[...] = jnp.zeros_like(acc_ref)
    m0 = jnp.full((BLOCK, 1), NEG, F32)
    l0 = jnp.zeros((BLOCK, 1), F32)
    state = []
    for hh in range(2):
        s = jnp.where(causal, scores(i, hh), NEG)
        state.extend(update(i, hh, s, m0, l0))

    def body(j, st):
        out = []
        for hh in range(2):
            out.extend(update(j, hh, scores(j, hh), st[2 * hh], st[2 * hh + 1]))
        return tuple(out)

    st = lax.fori_loop(0, i, body, tuple(state))
    o0 = acc_ref[0] * (1.0 / st[1])
    o1 = acc_ref[1] * (1.0 / st[3])
    out = jnp.where(lane < HEAD_DIM, o0, o1)
    o_ref[...] = (out * sga_ref[...].astype(F32)).astype(BF16)


def _attention(q, bias, k, v, sga, bsz, seq):
    n = bsz * seq
    nblk = seq // BLOCK
    npair = D_ATTN // LANES
    qrow = lambda b, p, i: (b * nblk + i, p)
    return pl.pallas_call(
        _attn_body,
        grid=(bsz, npair, nblk),
        in_specs=[pl.BlockSpec((BLOCK, LANES), qrow),
                  pl.BlockSpec((BLOCK, LANES), lambda b, p, i: (b * nblk + i, 0)),
                  pl.BlockSpec((seq, LANES), lambda b, p, i: (b, p)),
                  pl.BlockSpec((seq, LANES), lambda b, p, i: (b, p)),
                  pl.BlockSpec((BLOCK, LANES), qrow)],
        out_specs=pl.BlockSpec((BLOCK, LANES), qrow),
        out_shape=jax.ShapeDtypeStruct((n, D_ATTN), BF16),
        scratch_shapes=[pltpu.VMEM((2, BLOCK, LANES), F32)],
        compiler_params=pltpu.CompilerParams(dimension_semantics=("arbitrary",) * 3,
                                             vmem_limit_bytes=VMEM_LIMIT),
        name="attention",
    )(q, bias, k, v, sga)


def _outproj_body(x_ref, mod_ref, yc_ref, ya_ref, w_ref, gf_ref, o_ref):
    y = (jnp.dot(yc_ref[...], w_ref[0:D_CONV, :], preferred_element_type=F32)
         + jnp.dot(ya_ref[...], w_ref[D_CONV:, :], preferred_element_type=F32))
    gate = mod_ref[0, 2:3, :]
    z = x_ref[0] + gate * y
    r = lax.rsqrt(jnp.mean(z * z, axis=-1, keepdims=True) + EPS)
    o_ref[0] = z * r * gf_ref[...]


def _outproj(x, mod3, y_conv, y_attn, w_out_bf, g_final):
    bsz, seq, _ = x.shape
    nt = seq // OUT_TILE
    row = lambda b, t: (b * nt + t, 0)
    const = lambda b, t: (0, 0)
    xspec = pl.BlockSpec((1, OUT_TILE, D_MODEL), lambda b, t: (b, t, 0))
    return pl.pallas_call(
        _outproj_body,
        grid=(bsz, nt),
        in_specs=[xspec,
                  pl.BlockSpec((1, 3, D_MODEL), lambda b, t: (b, 0, 0)),
                  pl.BlockSpec((OUT_TILE, D_CONV), row),
                  pl.BlockSpec((OUT_TILE, D_ATTN), row),
                  pl.BlockSpec((D_MODEL, D_MODEL), const),
                  pl.BlockSpec((1, D_MODEL), const)],
        out_specs=xspec,
        out_shape=jax.ShapeDtypeStruct(x.shape, F32),
        compiler_params=pltpu.CompilerParams(dimension_semantics=("arbitrary", "arbitrary"),
                                             vmem_limit_bytes=VMEM_LIMIT),
        name="outproj",
    )(x, mod3, y_conv, y_attn, w_out_bf, g_final.reshape(1, -1))


def kernel(x, c, positions, w_ada, b_ada, g_norm, w_in, w_dw, b_dw, g_ln_conv, b_ln_conv, w_pw, b_pw,
           w_out, g_final):
    bsz, seq, _ = x.shape
    assert w_ada.shape[0] == 1, "single-layer block: the final rmsnorm is fused into the out-projection"
    cpat, spat = _rope_tables(positions)
    mod3 = _adaln(c, w_ada[0], b_ada[0]).reshape(bsz, 3, D_MODEL)
    w_hi = w_in[0].astype(BF16)
    w_lo = (w_in[0][:, C_Q:C_V] - w_hi[:, C_Q:C_V].astype(F32)).astype(BF16)
    u, sgc, q, k, v, sga, bias = _inproj(x, mod3, g_norm[0].reshape(1, -1), cpat, spat, w_hi, w_lo)
    y_conv = _conv(u, sgc, w_dw[0], b_dw[0], g_ln_conv[0], b_ln_conv[0], w_pw[0].astype(BF16), b_pw[0],
                   bsz, seq)
    y_attn = _attention(q, bias, k, v, sga, bsz, seq)
    return _outproj(x, mod3, y_conv, y_attn, w_out[0].astype(BF16), g_final)
```

```python
import jax
import jax.numpy as jnp
from jax import lax
from jax.experimental import pallas as pl
from jax.experimental.pallas import tpu as pltpu

F32 = jnp.float32
BF16 = jnp.bfloat16

D_MODEL = 1024
D_CONV = 512
D_ATTN = 512
HEAD_DIM = 64
N_HEADS = 8
CONV_WIDTH = 31
BLOCK = 256
TOP_K = 3
ROPE_DIM = 16
ROPE_HALF = 8
ROPE_THETA = 500000.0
EPS = 1e-6
NEG = -1e30
D_IN = 3584
C_VAL, C_GLU, C_GATE, C_Q, C_K, C_V, C_AG = 0, 512, 1024, 1536, 2048, 2560, 3072

LANES = 128
SEL_LANES = LANES // N_HEADS
HALO = 32
CONV_TILE = 256
CONV_CHUNK = 32
OUT_TILE = 512
VMEM_LIMIT = 48 * 1024 * 1024


def _split_bf16(a):
    hi = a.astype(BF16)
    lo = (a - hi.astype(F32)).astype(BF16)
    return hi, lo


def _dot(a, b):
    return jnp.dot(a, b, preferred_element_type=F32)


def _dot3(a_hi, a_lo, b_hi, b_lo):
    return _dot(a_hi, b_hi) + (_dot(a_hi, b_lo) + _dot(a_lo, b_hi))


def _sigmoid(a):
    return 1.0 / (1.0 + jnp.exp(-a))


def _silu(a):
    return a * _sigmoid(a)


def _adaln_body(c_ref, w_ref, b_ref, o_ref):
    a_hi, a_lo = _split_bf16(_silu(c_ref[...]))
    w_hi, w_lo = _split_bf16(w_ref[...])
    o_ref[...] = _dot3(a_hi, a_lo, w_hi, w_lo) + b_ref[...]


def _adaln(c, w_ada, b_ada):
    bsz = c.shape[0]
    n_out = w_ada.shape[1]
    tn = 1024
    return pl.pallas_call(
        _adaln_body,
        grid=(n_out // tn,),
        in_specs=[pl.BlockSpec((bsz, D_MODEL), lambda j: (0, 0)),
                  pl.BlockSpec((D_MODEL, tn), lambda j: (0, j)),
                  pl.BlockSpec((1, tn), lambda j: (0, j))],
        out_specs=pl.BlockSpec((bsz, tn), lambda j: (0, j)),
        out_shape=jax.ShapeDtypeStruct((bsz, n_out), F32),
        compiler_params=pltpu.CompilerParams(dimension_semantics=("arbitrary",),
                                             vmem_limit_bytes=VMEM_LIMIT),
        name="adaln",
    )(c, w_ada, b_ada.reshape(1, n_out))


def _rope_table_body(inv_ref, pos_ref, cos_ref, sin_ref):
    pos = pos_ref[...].astype(F32)
    for i in range(ROPE_HALF):
        ang = pos * inv_ref[i]
        cos_ref[i] = jnp.cos(ang)
        sin_ref[i] = jnp.sin(ang)


def _rope_tables(positions):
    bsz, seq = positions.shape
    inv = ROPE_THETA ** (-(jnp.arange(ROPE_HALF, dtype=F32) * 2.0) / ROPE_DIM)
    cos, sin = pl.pallas_call(
        _rope_table_body,
        in_specs=[pl.BlockSpec(memory_space=pltpu.SMEM),
                  pl.BlockSpec((bsz, seq), lambda: (0, 0))],
        out_specs=[pl.BlockSpec((ROPE_HALF, bsz, seq), lambda: (0, 0, 0))] * 2,
        out_shape=[jax.ShapeDtypeStruct((ROPE_HALF, bsz, seq), F32)] * 2,
        name="rope_table",
    )(inv, positions)
    return cos.reshape(ROPE_HALF, bsz * seq), sin.reshape(ROPE_HALF, bsz * seq)


def _rope_t(t, cos, sin):
    pieces = []
    for hd in range(N_HEADS):
        r0 = hd * HEAD_DIM
        t1 = t[r0:r0 + ROPE_HALF]
        t2 = t[r0 + ROPE_HALF:r0 + ROPE_DIM]
        pieces += [t1 * cos - t2 * sin, t2 * cos + t1 * sin, t[r0 + ROPE_DIM:r0 + HEAD_DIM]]
    return jnp.concatenate(pieces, axis=0)


def _inproj_body(x_ref, mod_ref, g_ref, cos_ref, sin_ref, whi_ref, wlo_ref,
                 u_ref, sgc_ref, qt_ref, k_ref, vt_ref, sga_ref, bias_ref, km_ref):
    i = pl.program_id(1)

    @pl.when(i == 0)
    def _():
        km_ref[...] = jnp.zeros_like(km_ref)

    x = x_ref[0]
    r = lax.rsqrt(jnp.mean(x * x, axis=-1, keepdims=True) + EPS)
    shift = mod_ref[0, 0:1, :]
    scale = mod_ref[0, 1:2, :]
    h = (x * r) * (g_ref[...] * (1.0 + scale)) + shift
    h_hi, h_lo = _split_bf16(h)

    def proj(c0):
        return _dot(h_hi, whi_ref[:, c0:c0 + 512])

    def proj3(c0, l0):
        return _dot3(h_hi, h_lo, whi_ref[:, c0:c0 + 512], wlo_ref[:, l0:l0 + 512])

    u_ref[...] = (proj(C_VAL) * _sigmoid(proj(C_GLU))).astype(BF16)
    sgc_ref[...] = _silu(proj(C_GATE)).astype(BF16)
    vt_ref[...] = proj(C_V).T.astype(BF16)
    sga_ref[...] = _silu(proj(C_AG)).astype(BF16)

    cos = cos_ref[...]
    sin = sin_ref[...]
    qt = _rope_t(proj3(C_Q, 0).T, cos, sin) * (HEAD_DIM ** -0.5)
    kr = _rope_t(proj3(C_K, 512).T, cos, sin).T
    qt_hi, qt_lo = _split_bf16(qt)
    qt_ref[...] = qt_hi
    k_ref[...] = kr.astype(BF16)

    km = km_ref[...]
    lane_k = lax.broadcasted_iota(jnp.int32, km.shape, 1)
    kmt = jnp.concatenate([jnp.where((lane_k // HEAD_DIM) == hd, km, 0.0) for hd in range(N_HEADS)], axis=0)
    kmt_hi, kmt_lo = _split_bf16(kmt)
    gt = _dot3(kmt_hi, kmt_lo, qt_hi, qt_lo)

    nb = km.shape[0]
    n_iota = lax.broadcasted_iota(jnp.int32, (nb, BLOCK), 0)
    past = n_iota < i
    n_sel = jnp.minimum(i, TOP_K)
    biases = []
    for hd in range(N_HEADS):
        g = jnp.where(past, gt[hd * nb:(hd + 1) * nb, :], NEG)
        rank = jnp.zeros((nb, BLOCK), jnp.int32)
        for n2 in range(nb):
            row = g[n2:n2 + 1, :]
            beats = (row > g) | ((row == g) & (n_iota > n2))
            rank = rank + jnp.where(beats, 1, 0)
        keep = (rank < n_sel) | (n_iota == i)
        biases.append(jnp.where(keep, 0.0, NEG))
    bias_ref[...] = jnp.concatenate(biases, axis=0).astype(BF16)

    km_ref[pl.ds(i, 1), :] = jnp.mean(kr, axis=0, keepdims=True)


def _inproj(x, mod3, g_norm, cos_t, sin_t, w_hi, w_lo):
    bsz, seq, _ = x.shape
    nblk = seq // BLOCK
    n = bsz * seq
    row = lambda b, i: (b * nblk + i, 0)
    col = lambda b, i: (0, b * nblk + i)
    const = lambda b, i: (0, 0)
    rows_out = pl.BlockSpec((BLOCK, 512), row)
    cols_out = pl.BlockSpec((512, BLOCK), col)
    rows_sd = jax.ShapeDtypeStruct((n, 512), BF16)
    cols_sd = jax.ShapeDtypeStruct((512, n), BF16)
    return pl.pallas_call(
        _inproj_body,
        grid=(bsz, nblk),
        in_specs=[pl.BlockSpec((1, BLOCK, D_MODEL), lambda b, i: (b, i, 0)),
                  pl.BlockSpec((1, 3, D_MODEL), lambda b, i: (b, 0, 0)),
                  pl.BlockSpec((1, D_MODEL), const),
                  pl.BlockSpec((ROPE_HALF, BLOCK), col),
                  pl.BlockSpec((ROPE_HALF, BLOCK), col),
                  pl.BlockSpec((D_MODEL, D_IN), const),
                  pl.BlockSpec((D_MODEL, 2 * D_ATTN), const)],
        out_specs=[rows_out, rows_out, cols_out, rows_out, cols_out, rows_out,
                   pl.BlockSpec((LANES, BLOCK), col)],
        out_shape=[rows_sd, rows_sd, cols_sd, rows_sd, cols_sd, rows_sd,
                   jax.ShapeDtypeStruct((LANES, n), BF16)],
        scratch_shapes=[pltpu.VMEM((nblk, D_ATTN), F32)],
        compiler_params=pltpu.CompilerParams(dimension_semantics=("arbitrary", "arbitrary"),
                                             vmem_limit_bytes=VMEM_LIMIT),
        name="inproj",
    )(x, mod3, g_norm, cos_t, sin_t, w_hi, w_lo)


def _conv_body(u_ref, halo_ref, sgc_ref, wdw_ref, bdw_ref, gln_ref, bln_ref, wpw_ref, bpw_ref,
               o_ref, win_ref, conv_ref):
    t = pl.program_id(1)
    halo = halo_ref[...].astype(F32)
    win_ref[0:HALO, :] = jnp.where(t == 0, 0.0, halo)
    win_ref[HALO:, :] = u_ref[...].astype(F32)
    base = HALO - (CONV_WIDTH - 1)
    for c in range(CONV_TILE // CONV_CHUNK):
        r0 = c * CONV_CHUNK
        acc = jnp.zeros((CONV_CHUNK, D_CONV), F32)
        for w in range(CONV_WIDTH):
            acc = acc + win_ref[r0 + base + w:r0 + base + w + CONV_CHUNK, :] * wdw_ref[w:w + 1, :]
        conv_ref[r0:r0 + CONV_CHUNK, :] = acc
    y = conv_ref[...] + bdw_ref[...]
    mu = jnp.mean(y, axis=-1, keepdims=True)
    yc = y - mu
    var = jnp.mean(yc * yc, axis=-1, keepdims=True)
    z = _silu(yc * lax.rsqrt(var + EPS) * gln_ref[...] + bln_ref[...])
    pw = _dot(z.astype(BF16), wpw_ref[...]) + bpw_ref[...]
    o_ref[...] = (pw * sgc_ref[...].astype(F32)).astype(BF16)


def _conv(u, sgc, w_dw, b_dw, g_ln, b_ln, w_pw_bf, b_pw, bsz, seq):
    n = bsz * seq
    nt = seq // CONV_TILE
    row = lambda b, t: (b * nt + t, 0)
    const = lambda b, t: (0, 0)
    halo_idx = lambda b, t: (jnp.maximum((b * seq + t * CONV_TILE) // HALO - 1, 0), 0)
    vec = pl.BlockSpec((1, D_CONV), const)
    return pl.pallas_call(
        _conv_body,
        grid=(bsz, nt),
        in_specs=[pl.BlockSpec((CONV_TILE, D_CONV), row),
                  pl.BlockSpec((HALO, D_CONV), halo_idx),
                  pl.BlockSpec((CONV_TILE, D_CONV), row),
                  pl.BlockSpec((CONV_WIDTH, D_CONV), const),
                  vec, vec, vec,
                  pl.BlockSpec((D_CONV, D_CONV), const),
                  vec],
        out_specs=pl.BlockSpec((CONV_TILE, D_CONV), row),
        out_shape=jax.ShapeDtypeStruct((n, D_CONV), BF16),
        scratch_shapes=[pltpu.VMEM((CONV_TILE + HALO, D_CONV), F32),
                        pltpu.VMEM((CONV_TILE, D_CONV), F32)],
        compiler_params=pltpu.CompilerParams(dimension_semantics=("arbitrary", "arbitrary"),
                                             vmem_limit_bytes=VMEM_LIMIT),
        name="conv",
    )(u, u, sgc, w_dw, b_dw.reshape(1, -1), g_ln.reshape(1, -1), b_ln.reshape(1, -1), w_pw_bf,
      b_pw.reshape(1, -1))


def _attn_body(qt_ref, bias_ref, k_ref, vt_ref, sga_ref, o_ref, qa_ref, m_ref, l_ref, acc_ref):
    i = pl.program_id(1)
    zeros_q = jnp.zeros((HEAD_DIM, BLOCK), BF16)
    for h in range(N_HEADS):
        qh = qt_ref[h * HEAD_DIM:(h + 1) * HEAD_DIM, :]
        bh = bias_ref[h * SEL_LANES:(h + 1) * SEL_LANES, :]
        pieces = [qh, zeros_q] if h % 2 == 0 else [zeros_q, qh]
        if h > 0:
            pieces.append(jnp.zeros((h * SEL_LANES, BLOCK), BF16))
        pieces.append(bh)
        if h < N_HEADS - 1:
            pieces.append(jnp.zeros(((N_HEADS - 1 - h) * SEL_LANES, BLOCK), BF16))
        qa_ref[h] = jnp.concatenate(pieces, axis=0)

    lane = lax.broadcasted_iota(jnp.int32, (BLOCK, LANES), 1)

    def key_operand(j, pair):
        start = pl.multiple_of(j * BLOCK, BLOCK)
        kj = k_ref[pl.ds(start, BLOCK), pair * LANES:(pair + 1) * LANES]
        onehot = jnp.where((lane % SEL_LANES) == j, 1.0, 0.0).astype(BF16)
        return jnp.concatenate([kj, onehot], axis=1)

    def head_step(j, h, st, m_prev, l_prev, first):
        start = pl.multiple_of(j * BLOCK, BLOCK)
        m_new = jnp.maximum(m_prev, jnp.max(st, axis=0, keepdims=True))
        pt = jnp.exp(st - m_new)
        alpha = jnp.exp(m_prev - m_new)
        l_new = alpha * l_prev + jnp.sum(pt, axis=0, keepdims=True)
        vth = vt_ref[h * HEAD_DIM:(h + 1) * HEAD_DIM, pl.ds(start, BLOCK)]
        pv = _dot(vth, pt.astype(BF16))
        rows = slice(h * HEAD_DIM, (h + 1) * HEAD_DIM)
        acc_ref[rows, :] = pv if first else alpha * acc_ref[rows, :] + pv
        m_ref[h:h + 1, :] = m_new
        l_ref[h:h + 1, :] = l_new

    key_idx = lax.broadcasted_iota(jnp.int32, (BLOCK, BLOCK), 0)
    qry_idx = lax.broadcasted_iota(jnp.int32, (BLOCK, BLOCK), 1)
    causal = key_idx <= qry_idx
    m0 = jnp.full((1, BLOCK), NEG, F32)
    l0 = jnp.zeros((1, BLOCK), F32)
    def all_scores(j):
        sts = []
        for pair in range(N_HEADS // 2):
            ka = key_operand(j, pair)
            sts += [_dot(ka, qa_ref[h]) for h in (2 * pair, 2 * pair + 1)]
        return sts

    sts = all_scores(i)
    for h in range(N_HEADS):
        head_step(i, h, jnp.where(causal, sts[h], NEG), m0, l0, True)

    def body(j, carry):
        sts = all_scores(j)
        for h in range(N_HEADS):
            head_step(j, h, sts[h], m_ref[h:h + 1, :], l_ref[h:h + 1, :], False)
        return carry

    lax.fori_loop(0, i, body, 0)

    outs = [acc_ref[h * HEAD_DIM:(h + 1) * HEAD_DIM, :] * (1.0 / l_ref[h:h + 1, :]) for h in range(N_HEADS)]
    out = jnp.concatenate(outs, axis=0).T
    o_ref[...] = (out * sga_ref[...].astype(F32)).astype(BF16)


def _attention(qt, bias_t, k, vt, sga, bsz, seq):
    n = bsz * seq
    nblk = seq // BLOCK
    row = lambda b, i: (b * nblk + i, 0)
    col = lambda b, i: (0, b * nblk + i)
    return pl.pallas_call(
        _attn_body,
        grid=(bsz, nblk),
        in_specs=[pl.BlockSpec((D_ATTN, BLOCK), col),
                  pl.BlockSpec((LANES, BLOCK), col),
                  pl.BlockSpec((seq, D_ATTN), lambda b, i: (b, 0)),
                  pl.BlockSpec((D_ATTN, seq), lambda b, i: (0, b)),
                  pl.BlockSpec((BLOCK, D_ATTN), row)],
        out_specs=pl.BlockSpec((BLOCK, D_ATTN), row),
        out_shape=jax.ShapeDtypeStruct((n, D_ATTN), BF16),
        scratch_shapes=[pltpu.VMEM((N_HEADS, 2 * LANES, BLOCK), BF16),
                        pltpu.VMEM((N_HEADS, BLOCK), F32),
                        pltpu.VMEM((N_HEADS, BLOCK), F32),
                        pltpu.VMEM((D_ATTN, BLOCK), F32)],
        compiler_params=pltpu.CompilerParams(dimension_semantics=("arbitrary", "arbitrary"),
                                             vmem_limit_bytes=VMEM_LIMIT),
        name="attention",
    )(qt, bias_t, k, vt, sga)


def _outproj_body(x_ref, mod_ref, yc_ref, ya_ref, w_ref, gf_ref, o_ref):
    y = _dot(yc_ref[...], w_ref[0:D_CONV, :]) + _dot(ya_ref[...], w_ref[D_CONV:, :])
    gate = mod_ref[0, 2:3, :]
    z = x_ref[0] + gate * y
    r = lax.rsqrt(jnp.mean(z * z, axis=-1, keepdims=True) + EPS)
    o_ref[0] = z * r * gf_ref[...]


def _outproj(x, mod3, y_conv, y_attn, w_out_bf, g_final):
    bsz, seq, _ = x.shape
    nt = seq // OUT_TILE
    row = lambda b, t: (b * nt + t, 0)
    const = lambda b, t: (0, 0)
    xspec = pl.BlockSpec((1, OUT_TILE, D_MODEL), lambda b, t: (b, t, 0))
    return pl.pallas_call(
        _outproj_body,
        grid=(bsz, nt),
        in_specs=[xspec,
                  pl.BlockSpec((1, 3, D_MODEL), lambda b, t: (b, 0, 0)),
                  pl.BlockSpec((OUT_TILE, D_CONV), row),
                  pl.BlockSpec((OUT_TILE, D_ATTN), row),
                  pl.BlockSpec((D_MODEL, D_MODEL), const),
                  pl.BlockSpec((1, D_MODEL), const)],
        out_specs=xspec,
        out_shape=jax.ShapeDtypeStruct(x.shape, F32),
        compiler_params=pltpu.CompilerParams(dimension_semantics=("arbitrary", "arbitrary"),
                                             vmem_limit_bytes=VMEM_LIMIT),
        name="outproj",
    )(x, mod3, y_conv, y_attn, w_out_bf, g_final.reshape(1, -1))


def kernel(x, c, positions, w_ada, b_ada, g_norm, w_in, w_dw, b_dw, g_ln_conv, b_ln_conv, w_pw, b_pw,
           w_out, g_final):
    bsz, seq, _ = x.shape
    assert w_ada.shape[0] == 1, "single-layer block: the final rmsnorm is fused into the out-projection"
    cos_t, sin_t = _rope_tables(positions)
    mod3 = _adaln(c, w_ada[0], b_ada[0]).reshape(bsz, 3, D_MODEL)
    w_hi = w_in[0].astype(BF16)
    w_lo = (w_in[0][:, C_Q:C_V] - w_hi[:, C_Q:C_V].astype(F32)).astype(BF16)
    u, sgc, qt, k, vt, sga, bias_t = _inproj(x, mod3, g_norm[0].reshape(1, -1), cos_t, sin_t, w_hi, w_lo)
    y_conv = _conv(u, sgc, w_dw[0], b_dw[0], g_ln_conv[0], b_ln_conv[0], w_pw[0].astype(BF16), b_pw[0],
                   bsz, seq)
    y_attn = _attention(qt, bias_t, k, vt, sga, bsz, seq)
    return _outproj(x, mod3, y_conv, y_attn, w_out[0].astype(BF16), g_final)
```

```python
import jax
import jax.numpy as jnp
from jax import lax
from jax.experimental import pallas as pl
from jax.experimental.pallas import tpu as pltpu

F32 = jnp.float32
BF16 = jnp.bfloat16

D_MODEL = 1024
D_CONV = 512
D_ATTN = 512
HEAD_DIM = 64
N_HEADS = 8
CONV_WIDTH = 31
BLOCK = 256
TOP_K = 3
ROPE_DIM = 16
ROPE_HALF = 8
ROPE_THETA = 500000.0
EPS = 1e-6
NEG = -1e30
Q_SCALE = HEAD_DIM ** -0.5 * 1.4426950408889634
D_IN = 3584
C_VAL, C_GLU, C_GATE, C_Q, C_K, C_V, C_AG = 0, 512, 1024, 1536, 2048, 2560, 3072

LANES = 128
SUBLANES = 8
SEL_LANES = LANES // N_HEADS
HALO = 32
CONV_TILE = 256
CONV_ROWS = CONV_TILE + HALO
CONV_CHUNK = 32
OUT_TILE = 512
VMEM_LIMIT = 48 * 1024 * 1024


def _split_bf16(a):
    hi = a.astype(BF16)
    lo = (a - hi.astype(F32)).astype(BF16)
    return hi, lo


def _dot(a, b):
    return jnp.dot(a, b, preferred_element_type=F32)


def _dot3(a_hi, a_lo, b_hi, b_lo):
    return _dot(a_hi, b_hi) + (_dot(a_hi, b_lo) + _dot(a_lo, b_hi))


def _sigmoid(a):
    return 1.0 / (1.0 + jnp.exp(-a))


def _silu(a):
    return a * _sigmoid(a)


def _adaln_body(c_ref, w_ref, b_ref, o_ref):
    a_hi, a_lo = _split_bf16(_silu(c_ref[...]))
    w_hi, w_lo = _split_bf16(w_ref[...])
    o_ref[...] = _dot3(a_hi, a_lo, w_hi, w_lo) + b_ref[...]


def _adaln(c, w_ada, b_ada):
    bsz = c.shape[0]
    n_out = w_ada.shape[1]
    tn = 1024
    return pl.pallas_call(
        _adaln_body,
        grid=(n_out // tn,),
        in_specs=[pl.BlockSpec((bsz, D_MODEL), lambda j: (0, 0)),
                  pl.BlockSpec((D_MODEL, tn), lambda j: (0, j)),
                  pl.BlockSpec((1, tn), lambda j: (0, j))],
        out_specs=pl.BlockSpec((bsz, tn), lambda j: (0, j)),
        out_shape=jax.ShapeDtypeStruct((bsz, n_out), F32),
        compiler_params=pltpu.CompilerParams(dimension_semantics=("arbitrary",),
                                             vmem_limit_bytes=VMEM_LIMIT),
        name="adaln",
    )(c, w_ada, b_ada.reshape(1, n_out))


def _rope_table_body(inv_ref, pos_ref, cos_ref, sin_ref):
    pos = pos_ref[...].astype(F32)
    for i in range(ROPE_HALF):
        ang = pos * inv_ref[i]
        cos_ref[i] = jnp.cos(ang)
        sin_ref[i] = jnp.sin(ang)


def _rope_tables(positions):
    bsz, seq = positions.shape
    inv = ROPE_THETA ** (-(jnp.arange(ROPE_HALF, dtype=F32) * 2.0) / ROPE_DIM)
    cos, sin = pl.pallas_call(
        _rope_table_body,
        in_specs=[pl.BlockSpec(memory_space=pltpu.SMEM),
                  pl.BlockSpec((bsz, seq), lambda: (0, 0))],
        out_specs=[pl.BlockSpec((ROPE_HALF, bsz, seq), lambda: (0, 0, 0))] * 2,
        out_shape=[jax.ShapeDtypeStruct((ROPE_HALF, bsz, seq), F32)] * 2,
        name="rope_table",
    )(inv, positions)
    return cos.reshape(ROPE_HALF, bsz * seq), sin.reshape(ROPE_HALF, bsz * seq)


def _rope_t(t, cos, sin):
    pieces = []
    for hd in range(N_HEADS):
        r0 = hd * HEAD_DIM
        t1 = t[r0:r0 + ROPE_HALF]
        t2 = t[r0 + ROPE_HALF:r0 + ROPE_DIM]
        pieces += [t1 * cos - t2 * sin, t2 * cos + t1 * sin, t[r0 + ROPE_DIM:r0 + HEAD_DIM]]
    return jnp.concatenate(pieces, axis=0)


def _inproj_body(x_ref, mod_ref, g_ref, cos_ref, sin_ref, whi_ref, wlo_ref,
                 u_ref, sgc_ref, qt_ref, k_ref, vt_ref, sga_ref, bias_ref, km_ref):
    i = pl.program_id(1)

    @pl.when(i == 0)
    def _():
        km_ref[...] = jnp.zeros_like(km_ref)

    x = x_ref[0]
    r = lax.rsqrt(jnp.mean(x * x, axis=-1, keepdims=True) + EPS)
    shift = mod_ref[0, 0:1, :]
    scale = mod_ref[0, 1:2, :]
    h = (x * r) * (g_ref[...] * (1.0 + scale)) + shift
    h_hi, h_lo = _split_bf16(h)

    def proj(c0):
        return _dot(h_hi, whi_ref[:, c0:c0 + 512])

    def proj3(c0, l0):
        return _dot3(h_hi, h_lo, whi_ref[:, c0:c0 + 512], wlo_ref[:, l0:l0 + 512])

    u_ref[...] = (proj(C_VAL) * _sigmoid(proj(C_GLU))).astype(BF16)
    sgc_ref[...] = _silu(proj(C_GATE)).astype(BF16)
    vt_ref[...] = proj(C_V).T.astype(BF16)
    sga_ref[...] = _silu(proj(C_AG)).astype(BF16)

    cos = cos_ref[...]
    sin = sin_ref[...]
    qt = _rope_t(proj3(C_Q, 0).T, cos, sin) * Q_SCALE
    kr = _rope_t(proj3(C_K, 512).T, cos, sin).T
    qt_hi, qt_lo = _split_bf16(qt)
    qt_ref[...] = qt_hi
    k_ref[...] = kr.astype(BF16)

    km = km_ref[...]
    lane_k = lax.broadcasted_iota(jnp.int32, km.shape, 1)
    kmt = jnp.concatenate([jnp.where((lane_k // HEAD_DIM) == hd, km, 0.0) for hd in range(N_HEADS)], axis=0)
    kmt_hi, kmt_lo = _split_bf16(kmt)
    gt = _dot3(kmt_hi, kmt_lo, qt_hi, qt_lo)

    nb = km.shape[0]
    n_iota = lax.broadcasted_iota(jnp.int32, (nb, BLOCK), 0)
    past = n_iota < i
    n_sel = jnp.minimum(i, TOP_K)
    biases = []
    for hd in range(N_HEADS):
        g = jnp.where(past, gt[hd * nb:(hd + 1) * nb, :], NEG)
        rank = jnp.zeros((nb, BLOCK), jnp.int32)
        for n2 in range(nb):
            row = g[n2:n2 + 1, :]
            beats = (row > g) | ((row == g) & (n_iota > n2))
            rank = rank + jnp.where(beats, 1, 0)
        biases.append(jnp.where(rank < n_sel, 0.0, NEG))
    bias_ref[...] = jnp.concatenate(biases, axis=0).astype(BF16)

    km_ref[pl.ds(i, 1), :] = jnp.mean(kr, axis=0, keepdims=True)


def _inproj(x, mod3, g_norm, cos_t, sin_t, w_hi, w_lo):
    bsz, seq, _ = x.shape
    nblk = seq // BLOCK
    n = bsz * seq
    row = lambda b, i: (b * nblk + i, 0)
    col = lambda b, i: (0, b * nblk + i)
    const = lambda b, i: (0, 0)
    rows_out = pl.BlockSpec((BLOCK, 512), row)
    cols_out = pl.BlockSpec((512, BLOCK), col)
    rows_sd = jax.ShapeDtypeStruct((n, 512), BF16)
    cols_sd = jax.ShapeDtypeStruct((512, n), BF16)
    return pl.pallas_call(
        _inproj_body,
        grid=(bsz, nblk),
        in_specs=[pl.BlockSpec((1, BLOCK, D_MODEL), lambda b, i: (b, i, 0)),
                  pl.BlockSpec((1, 3, D_MODEL), lambda b, i: (b, 0, 0)),
                  pl.BlockSpec((1, D_MODEL), const),
                  pl.BlockSpec((ROPE_HALF, BLOCK), col),
                  pl.BlockSpec((ROPE_HALF, BLOCK), col),
                  pl.BlockSpec((D_MODEL, D_IN), const),
                  pl.BlockSpec((D_MODEL, 2 * D_ATTN), const)],
        out_specs=[rows_out, rows_out, cols_out, rows_out, cols_out, rows_out,
                   pl.BlockSpec((LANES, BLOCK), col)],
        out_shape=[rows_sd, rows_sd, cols_sd, rows_sd, cols_sd, rows_sd,
                   jax.ShapeDtypeStruct((LANES, n), BF16)],
        scratch_shapes=[pltpu.VMEM((nblk, D_ATTN), F32)],
        compiler_params=pltpu.CompilerParams(dimension_semantics=("arbitrary", "arbitrary"),
                                             vmem_limit_bytes=VMEM_LIMIT),
        name="inproj",
    )(x, mod3, g_norm, cos_t, sin_t, w_hi, w_lo)


def _conv_body(u_ref, halo_ref, sgc_ref, wdw_ref, bdw_ref, gln_ref, bln_ref, wpw_ref, bpw_ref,
               o_ref, win_ref, conv_ref):
    t = pl.program_id(1)
    halo = halo_ref[...].astype(F32)
    win_ref[0, 0:HALO, :] = jnp.where(t == 0, 0.0, halo)
    win_ref[0, HALO:, :] = u_ref[...].astype(F32)
    for s in range(1, SUBLANES):
        win_ref[s, 0:CONV_ROWS - SUBLANES, :] = win_ref[0, s:s + CONV_ROWS - SUBLANES, :]
    base = HALO - (CONV_WIDTH - 1)
    for c in range(CONV_TILE // CONV_CHUNK):
        r0 = c * CONV_CHUNK
        acc = jnp.zeros((CONV_CHUNK, D_CONV), F32)
        for w in range(CONV_WIDTH):
            a, s = divmod(base + w, SUBLANES)
            rows = slice(r0 + a * SUBLANES, r0 + a * SUBLANES + CONV_CHUNK)
            tap = jnp.concatenate([wdw_ref[w]] * (CONV_CHUNK // SUBLANES), axis=0)
            acc = acc + win_ref[s, rows, :] * tap
        conv_ref[r0:r0 + CONV_CHUNK, :] = acc
    y = conv_ref[...] + bdw_ref[...]
    mu = jnp.mean(y, axis=-1, keepdims=True)
    yc = y - mu
    var = jnp.mean(yc * yc, axis=-1, keepdims=True)
    z = _silu(yc * lax.rsqrt(var + EPS) * gln_ref[...] + bln_ref[...])
    pw = _dot(z.astype(BF16), wpw_ref[...]) + bpw_ref[...]
    o_ref[...] = (pw * sgc_ref[...].astype(F32)).astype(BF16)


def _conv(u, sgc, w_dw, b_dw, g_ln, b_ln, w_pw_bf, b_pw, bsz, seq):
    n = bsz * seq
    nt = seq // CONV_TILE
    row = lambda b, t: (b * nt + t, 0)
    const = lambda b, t: (0, 0)
    halo_idx = lambda b, t: (jnp.maximum((b * seq + t * CONV_TILE) // HALO - 1, 0), 0)
    vec = pl.BlockSpec((1, D_CONV), const)
    return pl.pallas_call(
        _conv_body,
        grid=(bsz, nt),
        in_specs=[pl.BlockSpec((CONV_TILE, D_CONV), row),
                  pl.BlockSpec((HALO, D_CONV), halo_idx),
                  pl.BlockSpec((CONV_TILE, D_CONV), row),
                  pl.BlockSpec((CONV_WIDTH, SUBLANES, D_CONV), lambda b, t: (0, 0, 0)),
                  vec, vec, vec,
                  pl.BlockSpec((D_CONV, D_CONV), const),
                  vec],
        out_specs=pl.BlockSpec((CONV_TILE, D_CONV), row),
        out_shape=jax.ShapeDtypeStruct((n, D_CONV), BF16),
        scratch_shapes=[pltpu.VMEM((SUBLANES, CONV_ROWS, D_CONV), F32),
                        pltpu.VMEM((CONV_TILE, D_CONV), F32)],
        compiler_params=pltpu.CompilerParams(dimension_semantics=("arbitrary", "arbitrary"),
                                             vmem_limit_bytes=VMEM_LIMIT),
        name="conv",
    )(u, u, sgc, jnp.broadcast_to(w_dw[:, None, :], (CONV_WIDTH, SUBLANES, D_CONV)),
      b_dw.reshape(1, -1), g_ln.reshape(1, -1), b_ln.reshape(1, -1), w_pw_bf, b_pw.reshape(1, -1))


def _attn_body(qt_ref, bias_ref, k_ref, vt_ref, sga_ref, onehot_ref, o_ref,
               qa_ref, s_ref, m_ref, l_ref, acc_ref):
    i = pl.program_id(1)
    zeros_q = jnp.zeros((HEAD_DIM, BLOCK), BF16)
    for h in range(N_HEADS):
        qh = qt_ref[h * HEAD_DIM:(h + 1) * HEAD_DIM, :]
        bh = bias_ref[h * SEL_LANES:(h + 1) * SEL_LANES, :]
        pieces = [qh, zeros_q] if h % 2 == 0 else [zeros_q, qh]
        if h > 0:
            pieces.append(jnp.zeros((h * SEL_LANES, BLOCK), BF16))
        pieces.append(bh)
        if h < N_HEADS - 1:
            pieces.append(jnp.zeros(((N_HEADS - 1 - h) * SEL_LANES, BLOCK), BF16))
        qa_ref[h] = jnp.concatenate(pieces, axis=0)

    def scores(j, own):
        start = pl.multiple_of(j * BLOCK, BLOCK)
        sel = jnp.zeros((BLOCK, LANES), BF16) if own else onehot_ref[j]
        sts = []
        for pair in range(N_HEADS // 2):
            ka = jnp.concatenate([k_ref[pl.ds(start, BLOCK), pair * LANES:(pair + 1) * LANES], sel], axis=1)
            sts += [_dot(ka, qa_ref[h]) for h in (2 * pair, 2 * pair + 1)]
        return sts

    def scores_into(slot, j):
        for h, st in enumerate(scores(j, False)):
            s_ref[slot, h] = st

    def head_step(j, h, st, first):
        start = pl.multiple_of(j * BLOCK, BLOCK)
        m_cur = jnp.max(st, axis=0, keepdims=True)
        if first:
            m_new = m_cur
        else:
            m_prev = m_ref[h:h + 1, :]
            m_new = jnp.maximum(m_prev, m_cur)
            alpha = jnp.exp2(m_prev - m_new)
        pt = jnp.exp2(st - m_new)
        l_cur = jnp.sum(pt, axis=0, keepdims=True)
        vth = vt_ref[h * HEAD_DIM:(h + 1) * HEAD_DIM, pl.ds(start, BLOCK)]
        pv = _dot(vth, pt.astype(BF16))
        rows = slice(h * HEAD_DIM, (h + 1) * HEAD_DIM)
        acc_ref[rows, :] = pv if first else alpha * acc_ref[rows, :] + pv
        l_ref[h:h + 1, :] = l_cur if first else alpha * l_ref[h:h + 1, :] + l_cur
        m_ref[h:h + 1, :] = m_new

    key_idx = lax.broadcasted_iota(jnp.int32, (BLOCK, BLOCK), 0)
    qry_idx = lax.broadcasted_iota(jnp.int32, (BLOCK, BLOCK), 1)
    causal = key_idx <= qry_idx
    own = scores(i, True)
    scores_into(0, 0)
    for h in range(N_HEADS):
        head_step(i, h, jnp.where(causal, own[h], NEG), True)

    def pair_body(t, carry):
        j0 = 2 * t
        scores_into(1, j0 + 1)
        for h in range(N_HEADS):
            head_step(j0, h, s_ref[0, h], False)
        scores_into(0, jnp.minimum(j0 + 2, i))
        for h in range(N_HEADS):
            head_step(j0 + 1, h, s_ref[1, h], False)
        return carry

    lax.fori_loop(0, (i + 1) // 2, pair_body, 0)

    outs = [acc_ref[h * HEAD_DIM:(h + 1) * HEAD_DIM, :] * (1.0 / l_ref[h:h + 1, :]) for h in range(N_HEADS)]
    out = jnp.concatenate(outs, axis=0).T
    o_ref[...] = (out * sga_ref[...].astype(F32)).astype(BF16)


def _attention(qt, bias_t, k, vt, sga, bsz, seq):
    n = bsz * seq
    nblk = seq // BLOCK
    assert nblk == SEL_LANES, "one selection lane per key block"
    row = lambda b, i: (b * nblk + i, 0)
    col = lambda b, i: (0, b * nblk + i)
    lane = lax.broadcasted_iota(jnp.int32, (nblk, BLOCK, LANES), 2)
    blk = lax.broadcasted_iota(jnp.int32, (nblk, BLOCK, LANES), 0)
    onehot = ((lane % SEL_LANES) == blk).astype(BF16)
    return pl.pallas_call(
        _attn_body,
        grid=(bsz, nblk),
        in_specs=[pl.BlockSpec((D_ATTN, BLOCK), col),
                  pl.BlockSpec((LANES, BLOCK), col),
                  pl.BlockSpec((seq, D_ATTN), lambda b, i: (b, 0)),
                  pl.BlockSpec((D_ATTN, seq), lambda b, i: (0, b)),
                  pl.BlockSpec((BLOCK, D_ATTN), row),
                  pl.BlockSpec((nblk, BLOCK, LANES), lambda b, i: (0, 0, 0))],
        out_specs=pl.BlockSpec((BLOCK, D_ATTN), row),
        out_shape=jax.ShapeDtypeStruct((n, D_ATTN), BF16),
        scratch_shapes=[pltpu.VMEM((N_HEADS, 2 * LANES, BLOCK), BF16),
                        pltpu.VMEM((2, N_HEADS, BLOCK, BLOCK), F32),
                        pltpu.VMEM((N_HEADS, BLOCK), F32),
                        pltpu.VMEM((N_HEADS, BLOCK), F32),
                        pltpu.VMEM((D_ATTN, BLOCK), F32)],
        compiler_params=pltpu.CompilerParams(dimension_semantics=("arbitrary", "arbitrary"),
                                             vmem_limit_bytes=VMEM_LIMIT),
        name="attention",
    )(qt, bias_t, k, vt, sga, onehot)


def _outproj_body(x_ref, mod_ref, yc_ref, ya_ref, w_ref, gf_ref, o_ref):
    y = _dot(yc_ref[...], w_ref[0:D_CONV, :]) + _dot(ya_ref[...], w_ref[D_CONV:, :])
    gate = mod_ref[0, 2:3, :]
    z = x_ref[0] + gate * y
    r = lax.rsqrt(jnp.mean(z * z, axis=-1, keepdims=True) + EPS)
    o_ref[0] = z * r * gf_ref[...]


def _outproj(x, mod3, y_conv, y_attn, w_out_bf, g_final):
    bsz, seq, _ = x.shape
    nt = seq // OUT_TILE
    row = lambda b, t: (b * nt + t, 0)
    const = lambda b, t: (0, 0)
    xspec = pl.BlockSpec((1, OUT_TILE, D_MODEL), lambda b, t: (b, t, 0))
    return pl.pallas_call(
        _outproj_body,
        grid=(bsz, nt),
        in_specs=[xspec,
                  pl.BlockSpec((1, 3, D_MODEL), lambda b, t: (b, 0, 0)),
                  pl.BlockSpec((OUT_TILE, D_CONV), row),
                  pl.BlockSpec((OUT_TILE, D_ATTN), row),
                  pl.BlockSpec((D_MODEL, D_MODEL), const),
                  pl.BlockSpec((1, D_MODEL), const)],
        out_specs=xspec,
        out_shape=jax.ShapeDtypeStruct(x.shape, F32),
        compiler_params=pltpu.CompilerParams(dimension_semantics=("arbitrary", "arbitrary"),
                                             vmem_limit_bytes=VMEM_LIMIT),
        name="outproj",
    )(x, mod3, y_conv, y_attn, w_out_bf, g_final.reshape(1, -1))


def kernel(x, c, positions, w_ada, b_ada, g_norm, w_in, w_dw, b_dw, g_ln_conv, b_ln_conv, w_pw, b_pw,
           w_out, g_final):
    bsz, seq, _ = x.shape
    assert w_ada.shape[0] == 1, "single-layer block: the final rmsnorm is fused into the out-projection"
    cos_t, sin_t = _rope_tables(positions)
    mod3 = _adaln(c, w_ada[0], b_ada[0]).reshape(bsz, 3, D_MODEL)
    w_hi = w_in[0].astype(BF16)
    w_lo = (w_in[0][:, C_Q:C_V] - w_hi[:, C_Q:C_V].astype(F32)).astype(BF16)
    u, sgc, qt, k, vt, sga, bias_t = _inproj(x, mod3, g_norm[0].reshape(1, -1), cos_t, sin_t, w_hi, w_lo)
    y_conv = _conv(u, sgc, w_dw[0], b_dw[0], g_ln_conv[0], b_ln_conv[0], w_pw[0].astype(BF16), b_pw[0],
                   bsz, seq)
    y_attn = _attention(qt, bias_t, k, vt, sga, bsz, seq)
    return _outproj(x, mod3, y_conv, y_attn, w_out[0].astype(BF16), g_final)
```

```python
import jax
import jax.numpy as jnp
from jax import lax
from jax.experimental import pallas as pl
from jax.experimental.pallas import tpu as pltpu

F32 = jnp.float32
BF16 = jnp.bfloat16

D_MODEL = 1024
D_CONV = 512
D_ATTN = 512
HEAD_DIM = 64
N_HEADS = 8
CONV_WIDTH = 31
BLOCK = 256
TOP_K = 3
ROPE_DIM = 16
ROPE_HALF = 8
ROPE_THETA = 500000.0
EPS = 1e-6
NEG = -1e30
Q_SCALE = HEAD_DIM ** -0.5 * 1.4426950408889634
D_IN = 3584
C_VAL, C_GLU, C_GATE, C_Q, C_K, C_V, C_AG = 0, 512, 1024, 1536, 2048, 2560, 3072

LANES = 128
SUBLANES = 8
SEL_LANES = LANES // N_HEADS
V_ROWS = HEAD_DIM + 16
HALO = 32
CONV_TILE = 256
CONV_ROWS = CONV_TILE + HALO
CONV_CHUNK = 32
OUT_TILE = 512
VMEM_LIMIT = 48 * 1024 * 1024


def _split_bf16(a):
    hi = a.astype(BF16)
    lo = (a - hi.astype(F32)).astype(BF16)
    return hi, lo


def _dot(a, b):
    return jnp.dot(a, b, preferred_element_type=F32)


def _dot3(a_hi, a_lo, b_hi, b_lo):
    return _dot(a_hi, b_hi) + (_dot(a_hi, b_lo) + _dot(a_lo, b_hi))


def _sigmoid(a):
    return 1.0 / (1.0 + jnp.exp(-a))


def _silu(a):
    return a * _sigmoid(a)


def _adaln_body(c_ref, w_ref, b_ref, o_ref):
    a_hi, a_lo = _split_bf16(_silu(c_ref[...]))
    w_hi, w_lo = _split_bf16(w_ref[...])
    o_ref[...] = _dot3(a_hi, a_lo, w_hi, w_lo) + b_ref[...]


def _adaln(c, w_ada, b_ada):
    bsz = c.shape[0]
    n_out = w_ada.shape[1]
    tn = 1024
    return pl.pallas_call(
        _adaln_body,
        grid=(n_out // tn,),
        in_specs=[pl.BlockSpec((bsz, D_MODEL), lambda j: (0, 0)),
                  pl.BlockSpec((D_MODEL, tn), lambda j: (0, j)),
                  pl.BlockSpec((1, tn), lambda j: (0, j))],
        out_specs=pl.BlockSpec((bsz, tn), lambda j: (0, j)),
        out_shape=jax.ShapeDtypeStruct((bsz, n_out), F32),
        compiler_params=pltpu.CompilerParams(dimension_semantics=("arbitrary",),
                                             vmem_limit_bytes=VMEM_LIMIT),
        name="adaln",
    )(c, w_ada, b_ada.reshape(1, n_out))


def _rope_table_body(inv_ref, pos_ref, cos_ref, sin_ref):
    pos = pos_ref[...].astype(F32)
    for i in range(ROPE_HALF):
        ang = pos * inv_ref[i]
        cos_ref[i] = jnp.cos(ang)
        sin_ref[i] = jnp.sin(ang)


def _rope_tables(positions):
    bsz, seq = positions.shape
    inv = ROPE_THETA ** (-(jnp.arange(ROPE_HALF, dtype=F32) * 2.0) / ROPE_DIM)
    cos, sin = pl.pallas_call(
        _rope_table_body,
        in_specs=[pl.BlockSpec(memory_space=pltpu.SMEM),
                  pl.BlockSpec((bsz, seq), lambda: (0, 0))],
        out_specs=[pl.BlockSpec((ROPE_HALF, bsz, seq), lambda: (0, 0, 0))] * 2,
        out_shape=[jax.ShapeDtypeStruct((ROPE_HALF, bsz, seq), F32)] * 2,
        name="rope_table",
    )(inv, positions)
    return cos.reshape(ROPE_HALF, bsz * seq), sin.reshape(ROPE_HALF, bsz * seq)


def _rope_t(t, cos, sin):
    pieces = []
    for hd in range(N_HEADS):
        r0 = hd * HEAD_DIM
        t1 = t[r0:r0 + ROPE_HALF]
        t2 = t[r0 + ROPE_HALF:r0 + ROPE_DIM]
        pieces += [t1 * cos - t2 * sin, t2 * cos + t1 * sin, t[r0 + ROPE_DIM:r0 + HEAD_DIM]]
    return jnp.concatenate(pieces, axis=0)


def _inproj_body(x_ref, mod_ref, g_ref, cos_ref, sin_ref, whi_ref, wlo_ref,
                 u_ref, sgc_ref, qt_ref, k_ref, vt_ref, sga_ref, bias_ref, km_ref):
    i = pl.program_id(1)

    @pl.when(i == 0)
    def _():
        km_ref[...] = jnp.zeros_like(km_ref)

    x = x_ref[0]
    r = lax.rsqrt(jnp.mean(x * x, axis=-1, keepdims=True) + EPS)
    shift = mod_ref[0, 0:1, :]
    scale = mod_ref[0, 1:2, :]
    h = (x * r) * (g_ref[...] * (1.0 + scale)) + shift
    h_hi, h_lo = _split_bf16(h)

    def proj(c0):
        return _dot(h_hi, whi_ref[:, c0:c0 + 512])

    def proj3(c0, l0):
        return _dot3(h_hi, h_lo, whi_ref[:, c0:c0 + 512], wlo_ref[:, l0:l0 + 512])

    u_ref[...] = (proj(C_VAL) * _sigmoid(proj(C_GLU))).astype(BF16)
    sgc_ref[...] = _silu(proj(C_GATE)).astype(BF16)
    vt = proj(C_V).T.astype(BF16)
    ones = jnp.ones((V_ROWS - HEAD_DIM, BLOCK), BF16)
    vt_ref[...] = jnp.concatenate(
        [piece for hd in range(N_HEADS) for piece in (vt[hd * HEAD_DIM:(hd + 1) * HEAD_DIM], ones)], axis=0)
    sga_ref[...] = _silu(proj(C_AG)).astype(BF16)

    cos = cos_ref[...]
    sin = sin_ref[...]
    qt = _rope_t(proj3(C_Q, 0).T, cos, sin) * Q_SCALE
    kr = _rope_t(proj3(C_K, 512).T, cos, sin).T
    qt_hi, qt_lo = _split_bf16(qt)
    qt_ref[...] = qt_hi
    k_ref[...] = kr.astype(BF16)

    km = km_ref[...]
    lane_k = lax.broadcasted_iota(jnp.int32, km.shape, 1)
    kmt = jnp.concatenate([jnp.where((lane_k // HEAD_DIM) == hd, km, 0.0) for hd in range(N_HEADS)], axis=0)
    kmt_hi, kmt_lo = _split_bf16(kmt)
    gt = _dot3(kmt_hi, kmt_lo, qt_hi, qt_lo)

    nb = km.shape[0]
    n_iota = lax.broadcasted_iota(jnp.int32, (nb, BLOCK), 0)
    past = n_iota < i
    n_sel = jnp.minimum(i, TOP_K)
    biases = []
    for hd in range(N_HEADS):
        g = jnp.where(past, gt[hd * nb:(hd + 1) * nb, :], NEG)
        rank = jnp.zeros((nb, BLOCK), jnp.int32)
        for n2 in range(nb):
            row = g[n2:n2 + 1, :]
            beats = (row > g) | ((row == g) & (n_iota > n2))
            rank = rank + jnp.where(beats, 1, 0)
        biases.append(jnp.where(rank < n_sel, 0.0, NEG))
    bias_ref[...] = jnp.concatenate(biases, axis=0).astype(BF16)

    km_ref[pl.ds(i, 1), :] = jnp.mean(kr, axis=0, keepdims=True)


def _inproj(x, mod3, g_norm, cos_t, sin_t, w_hi, w_lo):
    bsz, seq, _ = x.shape
    nblk = seq // BLOCK
    n = bsz * seq
    row = lambda b, i: (b * nblk + i, 0)
    col = lambda b, i: (0, b * nblk + i)
    const = lambda b, i: (0, 0)
    rows_out = pl.BlockSpec((BLOCK, 512), row)
    cols_out = pl.BlockSpec((512, BLOCK), col)
    rows_sd = jax.ShapeDtypeStruct((n, 512), BF16)
    cols_sd = jax.ShapeDtypeStruct((512, n), BF16)
    return pl.pallas_call(
        _inproj_body,
        grid=(bsz, nblk),
        in_specs=[pl.BlockSpec((1, BLOCK, D_MODEL), lambda b, i: (b, i, 0)),
                  pl.BlockSpec((1, 3, D_MODEL), lambda b, i: (b, 0, 0)),
                  pl.BlockSpec((1, D_MODEL), const),
                  pl.BlockSpec((ROPE_HALF, BLOCK), col),
                  pl.BlockSpec((ROPE_HALF, BLOCK), col),
                  pl.BlockSpec((D_MODEL, D_IN), const),
                  pl.BlockSpec((D_MODEL, 2 * D_ATTN), const)],
        out_specs=[rows_out, rows_out, cols_out, rows_out, pl.BlockSpec((N_HEADS * V_ROWS, BLOCK), col), rows_out,
                   pl.BlockSpec((LANES, BLOCK), col)],
        out_shape=[rows_sd, rows_sd, cols_sd, rows_sd, jax.ShapeDtypeStruct((N_HEADS * V_ROWS, n), BF16), rows_sd,
                   jax.ShapeDtypeStruct((LANES, n), BF16)],
        scratch_shapes=[pltpu.VMEM((nblk, D_ATTN), F32)],
        compiler_params=pltpu.CompilerParams(dimension_semantics=("arbitrary", "arbitrary"),
                                             vmem_limit_bytes=VMEM_LIMIT),
        name="inproj",
    )(x, mod3, g_norm, cos_t, sin_t, w_hi, w_lo)


def _conv_body(u_ref, halo_ref, sgc_ref, wdw_ref, bdw_ref, gln_ref, bln_ref, wpw_ref, bpw_ref,
               o_ref, win_ref, conv_ref):
    t = pl.program_id(1)
    halo = halo_ref[...].astype(F32)
    win_ref[0, 0:HALO, :] = jnp.where(t == 0, 0.0, halo)
    win_ref[0, HALO:, :] = u_ref[...].astype(F32)
    for s in range(1, SUBLANES):
        win_ref[s, 0:CONV_ROWS - SUBLANES, :] = win_ref[0, s:s + CONV_ROWS - SUBLANES, :]
    base = HALO - (CONV_WIDTH - 1)
    for c in range(CONV_TILE // CONV_CHUNK):
        r0 = c * CONV_CHUNK
        acc = jnp.zeros((CONV_CHUNK, D_CONV), F32)
        for w in range(CONV_WIDTH):
            a, s = divmod(base + w, SUBLANES)
            rows = slice(r0 + a * SUBLANES, r0 + a * SUBLANES + CONV_CHUNK)
            tap = jnp.concatenate([wdw_ref[w]] * (CONV_CHUNK // SUBLANES), axis=0)
            acc = acc + win_ref[s, rows, :] * tap
        conv_ref[r0:r0 + CONV_CHUNK, :] = acc
    y = conv_ref[...] + bdw_ref[...]
    mu = jnp.mean(y, axis=-1, keepdims=True)
    yc = y - mu
    var = jnp.mean(yc * yc, axis=-1, keepdims=True)
    z = _silu(yc * lax.rsqrt(var + EPS) * gln_ref[...] + bln_ref[...])
    pw = _dot(z.astype(BF16), wpw_ref[...]) + bpw_ref[...]
    o_ref[...] = (pw * sgc_ref[...].astype(F32)).astype(BF16)


def _conv(u, sgc, w_dw, b_dw, g_ln, b_ln, w_pw_bf, b_pw, bsz, seq):
    n = bsz * seq
    nt = seq // CONV_TILE
    row = lambda b, t: (b * nt + t, 0)
    const = lambda b, t: (0, 0)
    halo_idx = lambda b, t: (jnp.maximum((b * seq + t * CONV_TILE) // HALO - 1, 0), 0)
    vec = pl.BlockSpec((1, D_CONV), const)
    return pl.pallas_call(
        _conv_body,
        grid=(bsz, nt),
        in_specs=[pl.BlockSpec((CONV_TILE, D_CONV), row),
                  pl.BlockSpec((HALO, D_CONV), halo_idx),
                  pl.BlockSpec((CONV_TILE, D_CONV), row),
                  pl.BlockSpec((CONV_WIDTH, SUBLANES, D_CONV), lambda b, t: (0, 0, 0)),
                  vec, vec, vec,
                  pl.BlockSpec((D_CONV, D_CONV), const),
                  vec],
        out_specs=pl.BlockSpec((CONV_TILE, D_CONV), row),
        out_shape=jax.ShapeDtypeStruct((n, D_CONV), BF16),
        scratch_shapes=[pltpu.VMEM((SUBLANES, CONV_ROWS, D_CONV), F32),
                        pltpu.VMEM((CONV_TILE, D_CONV), F32)],
        compiler_params=pltpu.CompilerParams(dimension_semantics=("arbitrary", "arbitrary"),
                                             vmem_limit_bytes=VMEM_LIMIT),
        name="conv",
    )(u, u, sgc, jnp.broadcast_to(w_dw[:, None, :], (CONV_WIDTH, SUBLANES, D_CONV)),
      b_dw.reshape(1, -1), g_ln.reshape(1, -1), b_ln.reshape(1, -1), w_pw_bf, b_pw.reshape(1, -1))


def _attn_body(qt_ref, bias_ref, k_ref, vt_ref, sga_ref, onehot_ref, o_ref,
               qa_ref, s_ref, m_ref, l_ref, acc_ref):
    i = pl.program_id(1)
    zeros_q = jnp.zeros((HEAD_DIM, BLOCK), BF16)
    for h in range(N_HEADS):
        qh = qt_ref[h * HEAD_DIM:(h + 1) * HEAD_DIM, :]
        bh = bias_ref[h * SEL_LANES:(h + 1) * SEL_LANES, :]
        pieces = [qh, zeros_q] if h % 2 == 0 else [zeros_q, qh]
        if h > 0:
            pieces.append(jnp.zeros((h * SEL_LANES, BLOCK), BF16))
        pieces.append(bh)
        if h < N_HEADS - 1:
            pieces.append(jnp.zeros(((N_HEADS - 1 - h) * SEL_LANES, BLOCK), BF16))
        qa_ref[h] = jnp.concatenate(pieces, axis=0)

    def score_fn(j, own):
        start = pl.multiple_of(j * BLOCK, BLOCK)
        sel = jnp.zeros((BLOCK, LANES), BF16) if own else onehot_ref[j]
        operands = {}

        def f(h):
            pair = h // 2
            if pair not in operands:
                kj = k_ref[pl.ds(start, BLOCK), pair * LANES:(pair + 1) * LANES]
                operands[pair] = jnp.concatenate([kj, sel], axis=1)
            return _dot(operands[pair], qa_ref[h])
        return f

    def head_step(j, h, st, first):
        start = pl.multiple_of(j * BLOCK, BLOCK)
        m_cur = jnp.max(st, axis=0, keepdims=True)
        if first:
            m_new = m_cur
        else:
            m_prev = m_ref[h:h + 1, :]
            m_new = jnp.maximum(m_prev, m_cur)
            alpha = jnp.exp2(m_prev - m_new)
        pt = jnp.exp2(st - m_new)
        vth = vt_ref[h * V_ROWS:(h + 1) * V_ROWS, pl.ds(start, BLOCK)]
        pv_l = _dot(vth, pt.astype(BF16))
        pv = pv_l[0:HEAD_DIM]
        l_cur = pv_l[HEAD_DIM:HEAD_DIM + 1]
        rows = slice(h * HEAD_DIM, (h + 1) * HEAD_DIM)
        acc_ref[rows, :] = pv if first else alpha * acc_ref[rows, :] + pv
        l_ref[h:h + 1, :] = l_cur if first else alpha * l_ref[h:h + 1, :] + l_cur
        m_ref[h:h + 1, :] = m_new

    key_idx = lax.broadcasted_iota(jnp.int32, (BLOCK, BLOCK), 0)
    qry_idx = lax.broadcasted_iota(jnp.int32, (BLOCK, BLOCK), 1)
    causal = key_idx <= qry_idx
    own_fn = score_fn(i, True)
    own = [own_fn(h) for h in range(N_HEADS)]

    def overlapped(slot_next, j_next, j_cur, cur, first):
        nxt = score_fn(j_next, False)
        for h in range(N_HEADS):
            s_ref[slot_next, h] = nxt(h)
            head_step(j_cur, h, cur(h), first)

    odd = i % 2
    overlapped(odd, 0, i, lambda h: jnp.where(causal, own[h], NEG), True)

    @pl.when(odd == 1)
    def _():
        overlapped(0, 1, 0, lambda h: s_ref[1, h], False)

    def pair_body(t, carry):
        j0 = odd + 2 * t
        overlapped(1, j0 + 1, j0, lambda h: s_ref[0, h], False)
        overlapped(0, jnp.minimum(j0 + 2, i), j0 + 1, lambda h: s_ref[1, h], False)
        return carry

    lax.fori_loop(0, i // 2, pair_body, 0)

    outs = [acc_ref[h * HEAD_DIM:(h + 1) * HEAD_DIM, :] * (1.0 / l_ref[h:h + 1, :]) for h in range(N_HEADS)]
    out = jnp.concatenate(outs, axis=0).T
    o_ref[...] = (out * sga_ref[...].astype(F32)).astype(BF16)


def _attention(qt, bias_t, k, vt, sga, bsz, seq):
    n = bsz * seq
    nblk = seq // BLOCK
    assert nblk == SEL_LANES, "one selection lane per key block"
    row = lambda b, i: (b * nblk + i, 0)
    col = lambda b, i: (0, b * nblk + i)
    lane = lax.broadcasted_iota(jnp.int32, (nblk, BLOCK, LANES), 2)
    blk = lax.broadcasted_iota(jnp.int32, (nblk, BLOCK, LANES), 0)
    onehot = ((lane % SEL_LANES) == blk).astype(BF16)
    return pl.pallas_call(
        _attn_body,
        grid=(bsz, nblk),
        in_specs=[pl.BlockSpec((D_ATTN, BLOCK), col),
                  pl.BlockSpec((LANES, BLOCK), col),
                  pl.BlockSpec((seq, D_ATTN), lambda b, i: (b, 0)),
                  pl.BlockSpec((N_HEADS * V_ROWS, seq), lambda b, i: (0, b)),
                  pl.BlockSpec((BLOCK, D_ATTN), row),
                  pl.BlockSpec((nblk, BLOCK, LANES), lambda b, i: (0, 0, 0))],
        out_specs=pl.BlockSpec((BLOCK, D_ATTN), row),
        out_shape=jax.ShapeDtypeStruct((n, D_ATTN), BF16),
        scratch_shapes=[pltpu.VMEM((N_HEADS, 2 * LANES, BLOCK), BF16),
                        pltpu.VMEM((2, N_HEADS, BLOCK, BLOCK), F32),
                        pltpu.VMEM((N_HEADS, BLOCK), F32),
                        pltpu.VMEM((N_HEADS, BLOCK), F32),
                        pltpu.VMEM((D_ATTN, BLOCK), F32)],
        compiler_params=pltpu.CompilerParams(dimension_semantics=("arbitrary", "arbitrary"),
                                             vmem_limit_bytes=VMEM_LIMIT),
        name="attention",
    )(qt, bias_t, k, vt, sga, onehot)


def _outproj_body(x_ref, mod_ref, yc_ref, ya_ref, w_ref, gf_ref, o_ref):
    y = _dot(yc_ref[...], w_ref[0:D_CONV, :]) + _dot(ya_ref[...], w_ref[D_CONV:, :])
    gate = mod_ref[0, 2:3, :]
    z = x_ref[0] + gate * y
    r = lax.rsqrt(jnp.mean(z * z, axis=-1, keepdims=True) + EPS)
    o_ref[0] = z * r * gf_ref[...]


def _outproj(x, mod3, y_conv, y_attn, w_out_bf, g_final):
    bsz, seq, _ = x.shape
    nt = seq // OUT_TILE
    row = lambda b, t: (b * nt + t, 0)
    const = lambda b, t: (0, 0)
    xspec = pl.BlockSpec((1, OUT_TILE, D_MODEL), lambda b, t: (b, t, 0))
    return pl.pallas_call(
        _outproj_body,
        grid=(bsz, nt),
        in_specs=[xspec,
                  pl.BlockSpec((1, 3, D_MODEL), lambda b, t: (b, 0, 0)),
                  pl.BlockSpec((OUT_TILE, D_CONV), row),
                  pl.BlockSpec((OUT_TILE, D_ATTN), row),
                  pl.BlockSpec((D_MODEL, D_MODEL), const),
                  pl.BlockSpec((1, D_MODEL), const)],
        out_specs=xspec,
        out_shape=jax.ShapeDtypeStruct(x.shape, F32),
        compiler_params=pltpu.CompilerParams(dimension_semantics=("arbitrary", "arbitrary"),
                                             vmem_limit_bytes=VMEM_LIMIT),
        name="outproj",
    )(x, mod3, y_conv, y_attn, w_out_bf, g_final.reshape(1, -1))


def kernel(x, c, positions, w_ada, b_ada, g_norm, w_in, w_dw, b_dw, g_ln_conv, b_ln_conv, w_pw, b_pw,
           w_out, g_final):
    bsz, seq, _ = x.shape
    assert w_ada.shape[0] == 1, "single-layer block: the final rmsnorm is fused into the out-projection"
    cos_t, sin_t = _rope_tables(positions)
    mod3 = _adaln(c, w_ada[0], b_ada[0]).reshape(bsz, 3, D_MODEL)
    w_hi = w_in[0].astype(BF16)
    w_lo = (w_in[0][:, C_Q:C_V] - w_hi[:, C_Q:C_V].astype(F32)).astype(BF16)
    u, sgc, qt, k, vt, sga, bias_t = _inproj(x, mod3, g_norm[0].reshape(1, -1), cos_t, sin_t, w_hi, w_lo)
    y_conv = _conv(u, sgc, w_dw[0], b_dw[0], g_ln_conv[0], b_ln_conv[0], w_pw[0].astype(BF16), b_pw[0],
                   bsz, seq)
    y_attn = _attention(qt, bias_t, k, vt, sga, bsz, seq)
    return _outproj(x, mod3, y_conv, y_attn, w_out[0].astype(BF16), g_final)
```

```python
import jax
import jax.numpy as jnp
from jax import lax
from jax.experimental import pallas as pl
from jax.experimental.pallas import tpu as pltpu

F32 = jnp.float32
BF16 = jnp.bfloat16

D_MODEL = 1024
D_CONV = 512
D_ATTN = 512
HEAD_DIM = 64
N_HEADS = 8
CONV_WIDTH = 31
BLOCK = 256
TOP_K = 3
ROPE_DIM = 16
ROPE_HALF = 8
ROPE_THETA = 500000.0
EPS = 1e-6
NEG = -1e30
Q_SCALE = HEAD_DIM ** -0.5 * 1.4426950408889634
D_IN = 3584
C_VAL, C_GLU, C_GATE, C_Q, C_K, C_V, C_AG = 0, 512, 1024, 1536, 2048, 2560, 3072

LANES = 128
SUBLANES = 8
SEL_LANES = LANES // N_HEADS
V_ROWS = HEAD_DIM + 16
HALO = 32
CONV_ROWS = BLOCK + HALO
CONV_CHUNK = 32
OUT_TILE = 512
VMEM_LIMIT = 48 * 1024 * 1024


def _split_bf16(a):
    hi = a.astype(BF16)
    lo = (a - hi.astype(F32)).astype(BF16)
    return hi, lo


def _dot(a, b):
    return jnp.dot(a, b, preferred_element_type=F32)


def _dot3(a_hi, a_lo, b_hi, b_lo):
    return _dot(a_hi, b_hi) + (_dot(a_hi, b_lo) + _dot(a_lo, b_hi))


def _sigmoid(a):
    return 1.0 / (1.0 + jnp.exp(-a))


def _silu(a):
    return a * _sigmoid(a)


def _adaln_body(c_ref, w_ref, b_ref, o_ref):
    a_hi, a_lo = _split_bf16(_silu(c_ref[...]))
    w_hi, w_lo = _split_bf16(w_ref[...])
    o_ref[...] = _dot3(a_hi, a_lo, w_hi, w_lo) + b_ref[...]


def _adaln(c, w_ada, b_ada):
    bsz = c.shape[0]
    n_out = w_ada.shape[1]
    tn = 1024
    return pl.pallas_call(
        _adaln_body,
        grid=(n_out // tn,),
        in_specs=[pl.BlockSpec((bsz, D_MODEL), lambda j: (0, 0)),
                  pl.BlockSpec((D_MODEL, tn), lambda j: (0, j)),
                  pl.BlockSpec((1, tn), lambda j: (0, j))],
        out_specs=pl.BlockSpec((bsz, tn), lambda j: (0, j)),
        out_shape=jax.ShapeDtypeStruct((bsz, n_out), F32),
        compiler_params=pltpu.CompilerParams(dimension_semantics=("arbitrary",),
                                             vmem_limit_bytes=VMEM_LIMIT),
        name="adaln",
    )(c, w_ada, b_ada.reshape(1, n_out))


def _rope_table_body(inv_ref, pos_ref, cos_ref, sin_ref):
    pos = pos_ref[...].astype(F32)
    for i in range(ROPE_HALF):
        ang = pos * inv_ref[i]
        cos_ref[i] = jnp.cos(ang)
        sin_ref[i] = jnp.sin(ang)


def _rope_tables(positions):
    bsz, seq = positions.shape
    inv = ROPE_THETA ** (-(jnp.arange(ROPE_HALF, dtype=F32) * 2.0) / ROPE_DIM)
    cos, sin = pl.pallas_call(
        _rope_table_body,
        in_specs=[pl.BlockSpec(memory_space=pltpu.SMEM),
                  pl.BlockSpec((bsz, seq), lambda: (0, 0))],
        out_specs=[pl.BlockSpec((ROPE_HALF, bsz, seq), lambda: (0, 0, 0))] * 2,
        out_shape=[jax.ShapeDtypeStruct((ROPE_HALF, bsz, seq), F32)] * 2,
        name="rope_table",
    )(inv, positions)
    return cos.reshape(ROPE_HALF, bsz * seq), sin.reshape(ROPE_HALF, bsz * seq)


def _rope_t(t, cos, sin):
    pieces = []
    for hd in range(N_HEADS):
        r0 = hd * HEAD_DIM
        t1 = t[r0:r0 + ROPE_HALF]
        t2 = t[r0 + ROPE_HALF:r0 + ROPE_DIM]
        pieces += [t1 * cos - t2 * sin, t2 * cos + t1 * sin, t[r0 + ROPE_DIM:r0 + HEAD_DIM]]
    return jnp.concatenate(pieces, axis=0)


def _conv_steps(u, win_ref, conv_ref, wdw_ref):
    def setup():
        win_ref[0, HALO:, :] = u
        for s in range(1, SUBLANES):
            win_ref[s, 0:CONV_ROWS - SUBLANES, :] = win_ref[0, s:s + CONV_ROWS - SUBLANES, :]

    def chunk(c, after):
        r0 = c * CONV_CHUNK
        base = HALO - (CONV_WIDTH - 1)
        acc = jnp.minimum(jnp.maximum(after[0:CONV_CHUNK, :], 0.0), 0.0)
        for w in range(CONV_WIDTH):
            a, s = divmod(base + w, SUBLANES)
            rows = slice(r0 + a * SUBLANES, r0 + a * SUBLANES + CONV_CHUNK)
            tap = jnp.concatenate([wdw_ref[w]] * (CONV_CHUNK // SUBLANES), axis=0)
            acc = acc + win_ref[s, rows, :] * tap
        conv_ref[r0:r0 + CONV_CHUNK, :] = acc

    return [lambda after: setup()] + [lambda after, c=c: chunk(c, after) for c in range(BLOCK // CONV_CHUNK)]


def _conv_finish(gate_act, win_ref, conv_ref, bdw_ref, gln_ref, bln_ref, wpw_ref, bpw_ref):
    win_ref[0, 0:HALO, :] = win_ref[0, BLOCK:BLOCK + HALO, :]
    y = conv_ref[...] + bdw_ref[...]
    mu = jnp.mean(y, axis=-1, keepdims=True)
    yc = y - mu
    var = jnp.mean(yc * yc, axis=-1, keepdims=True)
    z = _silu(yc * lax.rsqrt(var + EPS) * gln_ref[...] + bln_ref[...])
    return (_dot(z.astype(BF16), wpw_ref[...]) + bpw_ref[...]) * gate_act


def _inproj_body(x_ref, mod_ref, g_ref, cos_ref, sin_ref, whi_ref, wlo_ref,
                 wdw_ref, bdw_ref, gln_ref, bln_ref, wpw_ref, bpw_ref,
                 yc_ref, qt_ref, k_ref, vt_ref, sga_ref, bias_ref,
                 km_ref, win_ref, conv_ref):
    i = pl.program_id(1)

    @pl.when(i == 0)
    def _():
        km_ref[...] = jnp.zeros_like(km_ref)
        win_ref[0, 0:HALO, :] = jnp.zeros((HALO, D_CONV), F32)

    x = x_ref[0]
    r = lax.rsqrt(jnp.mean(x * x, axis=-1, keepdims=True) + EPS)
    shift = mod_ref[0, 0:1, :]
    scale = mod_ref[0, 1:2, :]
    h = (x * r) * (g_ref[...] * (1.0 + scale)) + shift
    h_hi, h_lo = _split_bf16(h)

    def w_hi(c0):
        return whi_ref[:, c0:c0 + 512]

    def w_lo(c0):
        return wlo_ref[:, c0 - C_Q:c0 - C_Q + 512]

    u = _dot(h_hi, w_hi(C_VAL)) * _sigmoid(_dot(h_hi, w_hi(C_GLU)))
    matmuls = [lambda: _dot(h_hi, w_hi(C_GATE)), lambda: _dot(h_hi, w_hi(C_V)), lambda: _dot(h_hi, w_hi(C_AG)),
               lambda: _dot(h_hi, w_hi(C_Q)), lambda: _dot(h_hi, w_lo(C_Q)), lambda: _dot(h_lo, w_hi(C_Q)),
               lambda: _dot(h_hi, w_hi(C_K)), lambda: _dot(h_hi, w_lo(C_K)), lambda: _dot(h_lo, w_hi(C_K))]
    products = []
    for matmul, conv_step in zip(matmuls, _conv_steps(u, win_ref, conv_ref, wdw_ref), strict=True):
        products.append(matmul())
        conv_step(products[-1])
    c_gate, v, a_gate, q_hh, q_hl, q_lh, k_hh, k_hl, k_lh = products
    y_conv = _conv_finish(_silu(c_gate), win_ref, conv_ref, bdw_ref, gln_ref, bln_ref, wpw_ref, bpw_ref)
    yc_ref[...] = y_conv.astype(BF16)
    q = q_hh + (q_hl + q_lh)
    k = k_hh + (k_hl + k_lh)

    vt = v.T.astype(BF16)
    ones = jnp.ones((V_ROWS - HEAD_DIM, BLOCK), BF16)
    vt_ref[...] = jnp.concatenate(
        [piece for hd in range(N_HEADS) for piece in (vt[hd * HEAD_DIM:(hd + 1) * HEAD_DIM], ones)], axis=0)
    sga_ref[...] = _silu(a_gate).astype(BF16)

    cos = cos_ref[...]
    sin = sin_ref[...]
    qt = _rope_t(q.T, cos, sin) * Q_SCALE
    kr = _rope_t(k.T, cos, sin).T
    qt_hi, qt_lo = _split_bf16(qt)
    qt_ref[...] = qt_hi
    k_ref[...] = kr.astype(BF16)

    km = km_ref[...]
    lane_k = lax.broadcasted_iota(jnp.int32, km.shape, 1)
    kmt = jnp.concatenate([jnp.where((lane_k // HEAD_DIM) == hd, km, 0.0) for hd in range(N_HEADS)], axis=0)
    kmt_hi, kmt_lo = _split_bf16(kmt)
    gt = _dot3(kmt_hi, kmt_lo, qt_hi, qt_lo)

    nb = km.shape[0]
    n_iota = lax.broadcasted_iota(jnp.int32, (nb, BLOCK), 0)
    past = n_iota < i
    n_sel = jnp.minimum(i, TOP_K)
    biases = []
    for hd in range(N_HEADS):
        g = jnp.where(past, gt[hd * nb:(hd + 1) * nb, :], NEG)
        rank = jnp.zeros((nb, BLOCK), jnp.int32)
        for n2 in range(nb):
            row = g[n2:n2 + 1, :]
            beats = (row > g) | ((row == g) & (n_iota > n2))
            rank = rank + jnp.where(beats, 1, 0)
        biases.append(jnp.where(rank < n_sel, 0.0, NEG))
    bias_ref[...] = jnp.concatenate(biases, axis=0).astype(BF16)

    km_ref[pl.ds(i, 1), :] = jnp.mean(kr, axis=0, keepdims=True)


def _inproj(x, mod3, g_norm, cos_t, sin_t, w_hi, w_lo, w_dw, b_dw, g_ln, b_ln, w_pw_bf, b_pw):
    bsz, seq, _ = x.shape
    nblk = seq // BLOCK
    n = bsz * seq
    row = lambda b, i: (b * nblk + i, 0)
    col = lambda b, i: (0, b * nblk + i)
    const = lambda b, i: (0, 0)
    rows_out = pl.BlockSpec((BLOCK, 512), row)
    rows_sd = jax.ShapeDtypeStruct((n, 512), BF16)
    vec = pl.BlockSpec((1, D_CONV), const)
    return pl.pallas_call(
        _inproj_body,
        grid=(bsz, nblk),
        in_specs=[pl.BlockSpec((1, BLOCK, D_MODEL), lambda b, i: (b, i, 0)),
                  pl.BlockSpec((1, 3, D_MODEL), lambda b, i: (b, 0, 0)),
                  pl.BlockSpec((1, D_MODEL), const),
                  pl.BlockSpec((ROPE_HALF, BLOCK), col),
                  pl.BlockSpec((ROPE_HALF, BLOCK), col),
                  pl.BlockSpec((D_MODEL, D_IN), const),
                  pl.BlockSpec((D_MODEL, 2 * D_ATTN), const),
                  pl.BlockSpec((CONV_WIDTH, SUBLANES, D_CONV), lambda b, i: (0, 0, 0)),
                  vec, vec, vec,
                  pl.BlockSpec((D_CONV, D_CONV), const),
                  vec],
        out_specs=[rows_out,
                   pl.BlockSpec((D_ATTN, BLOCK), col),
                   rows_out,
                   pl.BlockSpec((N_HEADS * V_ROWS, BLOCK), col),
                   rows_out,
                   pl.BlockSpec((LANES, BLOCK), col)],
        out_shape=[rows_sd,
                   jax.ShapeDtypeStruct((D_ATTN, n), BF16),
                   rows_sd,
                   jax.ShapeDtypeStruct((N_HEADS * V_ROWS, n), BF16),
                   rows_sd,
                   jax.ShapeDtypeStruct((LANES, n), BF16)],
        scratch_shapes=[pltpu.VMEM((nblk, D_ATTN), F32),
                        pltpu.VMEM((SUBLANES, CONV_ROWS, D_CONV), F32),
                        pltpu.VMEM((BLOCK, D_CONV), F32)],
        compiler_params=pltpu.CompilerParams(dimension_semantics=("arbitrary", "arbitrary"),
                                             vmem_limit_bytes=VMEM_LIMIT),
        name="inproj",
    )(x, mod3, g_norm, cos_t, sin_t, w_hi, w_lo,
      jnp.broadcast_to(w_dw[:, None, :], (CONV_WIDTH, SUBLANES, D_CONV)),
      b_dw.reshape(1, -1), g_ln.reshape(1, -1), b_ln.reshape(1, -1), w_pw_bf, b_pw.reshape(1, -1))


def _attn_body(qt_ref, bias_ref, k_ref, vt_ref, sga_ref, onehot_ref, o_ref,
               qa_ref, s_ref, m_ref, l_ref, acc_ref):
    i = pl.program_id(1)
    zeros_q = jnp.zeros((HEAD_DIM, BLOCK), BF16)
    for h in range(N_HEADS):
        qh = qt_ref[h * HEAD_DIM:(h + 1) * HEAD_DIM, :]
        bh = bias_ref[h * SEL_LANES:(h + 1) * SEL_LANES, :]
        pieces = [qh, zeros_q] if h % 2 == 0 else [zeros_q, qh]
        if h > 0:
            pieces.append(jnp.zeros((h * SEL_LANES, BLOCK), BF16))
        pieces.append(bh)
        if h < N_HEADS - 1:
            pieces.append(jnp.zeros(((N_HEADS - 1 - h) * SEL_LANES, BLOCK), BF16))
        qa_ref[h] = jnp.concatenate(pieces, axis=0)

    def score_fn(j, own):
        start = pl.multiple_of(j * BLOCK, BLOCK)
        sel = jnp.zeros((BLOCK, LANES), BF16) if own else onehot_ref[j]
        operands = {}

        def f(h):
            pair = h // 2
            if pair not in operands:
                kj = k_ref[pl.ds(start, BLOCK), pair * LANES:(pair + 1) * LANES]
                operands[pair] = jnp.concatenate([kj, sel], axis=1)
            return _dot(operands[pair], qa_ref[h])
        return f

    def head_step(j, h, st, first):
        start = pl.multiple_of(j * BLOCK, BLOCK)
        m_cur = jnp.max(st, axis=0, keepdims=True)
        if first:
            m_new = m_cur
        else:
            m_prev = m_ref[h:h + 1, :]
            m_new = jnp.maximum(m_prev, m_cur)
            alpha = jnp.exp2(m_prev - m_new)
        pt = jnp.exp2(st - m_new)
        vth = vt_ref[h * V_ROWS:(h + 1) * V_ROWS, pl.ds(start, BLOCK)]
        pv_l = _dot(vth, pt.astype(BF16))
        pv = pv_l[0:HEAD_DIM]
        l_cur = pv_l[HEAD_DIM:HEAD_DIM + 1]
        rows = slice(h * HEAD_DIM, (h + 1) * HEAD_DIM)
        acc_ref[rows, :] = pv if first else alpha * acc_ref[rows, :] + pv
        l_ref[h:h + 1, :] = l_cur if first else alpha * l_ref[h:h + 1, :] + l_cur
        m_ref[h:h + 1, :] = m_new

    key_idx = lax.broadcasted_iota(jnp.int32, (BLOCK, BLOCK), 0)
    qry_idx = lax.broadcasted_iota(jnp.int32, (BLOCK, BLOCK), 1)
    causal = key_idx <= qry_idx
    own_fn = score_fn(i, True)
    own = [own_fn(h) for h in range(N_HEADS)]

    def overlapped(slot_next, j_next, j_cur, cur, first):
        nxt = score_fn(j_next, False)
        for h in range(N_HEADS):
            s_ref[slot_next, h] = nxt(h)
            head_step(j_cur, h, cur(h), first)

    odd = i % 2
    overlapped(odd, 0, i, lambda h: jnp.where(causal, own[h], NEG), True)

    @pl.when(odd == 1)
    def _():
        overlapped(0, 1, 0, lambda h: s_ref[1, h], False)

    def pair_body(t, carry):
        j0 = odd + 2 * t
        overlapped(1, j0 + 1, j0, lambda h: s_ref[0, h], False)
        overlapped(0, jnp.minimum(j0 + 2, i), j0 + 1, lambda h: s_ref[1, h], False)
        return carry

    lax.fori_loop(0, i // 2, pair_body, 0)

    outs = [acc_ref[h * HEAD_DIM:(h + 1) * HEAD_DIM, :] * (1.0 / l_ref[h:h + 1, :]) for h in range(N_HEADS)]
    out = jnp.concatenate(outs, axis=0).T
    o_ref[...] = (out * sga_ref[...].astype(F32)).astype(BF16)


def _attention(qt, bias_t, k, vt, sga, bsz, seq):
    n = bsz * seq
    nblk = seq // BLOCK
    assert nblk == SEL_LANES, "one selection lane per key block"
    row = lambda b, i: (b * nblk + i, 0)
    col = lambda b, i: (0, b * nblk + i)
    lane = lax.broadcasted_iota(jnp.int32, (nblk, BLOCK, LANES), 2)
    blk = lax.broadcasted_iota(jnp.int32, (nblk, BLOCK, LANES), 0)
    onehot = ((lane % SEL_LANES) == blk).astype(BF16)
    return pl.pallas_call(
        _attn_body,
        grid=(bsz, nblk),
        in_specs=[pl.BlockSpec((D_ATTN, BLOCK), col),
                  pl.BlockSpec((LANES, BLOCK), col),
                  pl.BlockSpec((seq, D_ATTN), lambda b, i: (b, 0)),
                  pl.BlockSpec((N_HEADS * V_ROWS, seq), lambda b, i: (0, b)),
                  pl.BlockSpec((BLOCK, D_ATTN), row),
                  pl.BlockSpec((nblk, BLOCK, LANES), lambda b, i: (0, 0, 0))],
        out_specs=pl.BlockSpec((BLOCK, D_ATTN), row),
        out_shape=jax.ShapeDtypeStruct((n, D_ATTN), BF16),
        scratch_shapes=[pltpu.VMEM((N_HEADS, 2 * LANES, BLOCK), BF16),
                        pltpu.VMEM((2, N_HEADS, BLOCK, BLOCK), F32),
                        pltpu.VMEM((N_HEADS, BLOCK), F32),
                        pltpu.VMEM((N_HEADS, BLOCK), F32),
                        pltpu.VMEM((D_ATTN, BLOCK), F32)],
        compiler_params=pltpu.CompilerParams(dimension_semantics=("arbitrary", "arbitrary"),
                                             vmem_limit_bytes=VMEM_LIMIT),
        name="attention",
    )(qt, bias_t, k, vt, sga, onehot)


def _outproj_body(x_ref, mod_ref, yc_ref, ya_ref, w_ref, gf_ref, o_ref):
    y = _dot(yc_ref[...], w_ref[0:D_CONV, :]) + _dot(ya_ref[...], w_ref[D_CONV:, :])
    gate = mod_ref[0, 2:3, :]
    z = x_ref[0] + gate * y
    r = lax.rsqrt(jnp.mean(z * z, axis=-1, keepdims=True) + EPS)
    o_ref[0] = z * r * gf_ref[...]


def _outproj(x, mod3, y_conv, y_attn, w_out_bf, g_final):
    bsz, seq, _ = x.shape
    nt = seq // OUT_TILE
    row = lambda b, t: (b * nt + t, 0)
    const = lambda b, t: (0, 0)
    xspec = pl.BlockSpec((1, OUT_TILE, D_MODEL), lambda b, t: (b, t, 0))
    return pl.pallas_call(
        _outproj_body,
        grid=(bsz, nt),
        in_specs=[xspec,
                  pl.BlockSpec((1, 3, D_MODEL), lambda b, t: (b, 0, 0)),
                  pl.BlockSpec((OUT_TILE, D_CONV), row),
                  pl.BlockSpec((OUT_TILE, D_ATTN), row),
                  pl.BlockSpec((D_MODEL, D_MODEL), const),
                  pl.BlockSpec((1, D_MODEL), const)],
        out_specs=xspec,
        out_shape=jax.ShapeDtypeStruct(x.shape, F32),
        compiler_params=pltpu.CompilerParams(dimension_semantics=("arbitrary", "arbitrary"),
                                             vmem_limit_bytes=VMEM_LIMIT),
        name="outproj",
    )(x, mod3, y_conv, y_attn, w_out_bf, g_final.reshape(1, -1))


def kernel(x, c, positions, w_ada, b_ada, g_norm, w_in, w_dw, b_dw, g_ln_conv, b_ln_conv, w_pw, b_pw,
           w_out, g_final):
    bsz, seq, _ = x.shape
    assert w_ada.shape[0] == 1, "single-layer block: the final rmsnorm is fused into the out-projection"
    cos_t, sin_t = _rope_tables(positions)
    mod3 = _adaln(c, w_ada[0], b_ada[0]).reshape(bsz, 3, D_MODEL)
    w_hi = w_in[0].astype(BF16)
    w_lo = (w_in[0][:, C_Q:C_V] - w_hi[:, C_Q:C_V].astype(F32)).astype(BF16)
    y_conv, qt, k, vt, sga, bias_t = _inproj(x, mod3, g_norm[0].reshape(1, -1), cos_t, sin_t, w_hi, w_lo,
                                             w_dw[0], b_dw[0], g_ln_conv[0], b_ln_conv[0],
                                             w_pw[0].astype(BF16), b_pw[0])
    y_attn = _attention(qt, bias_t, k, vt, sga, bsz, seq)
    return _outproj(x, mod3, y_conv, y_attn, w_out[0].astype(BF16), g_final)
```

```python
import jax
import jax.numpy as jnp
from jax import lax
from jax.experimental import pallas as pl
from jax.experimental.pallas import tpu as pltpu

F32 = jnp.float32
BF16 = jnp.bfloat16

D_MODEL = 1024
D_CONV = 512
D_ATTN = 512
HEAD_DIM = 64
N_HEADS = 8
CONV_WIDTH = 31
BLOCK = 256
TOP_K = 3
ROPE_DIM = 16
ROPE_HALF = 8
ROPE_THETA = 500000.0
EPS = 1e-6
NEG = -1e30
Q_SCALE = HEAD_DIM ** -0.5 * 1.4426950408889634
D_IN = 3584
C_VAL, C_GLU, C_GATE, C_Q, C_K, C_V, C_AG = 0, 512, 1024, 1536, 2048, 2560, 3072

LANES = 128
SUBLANES = 8
SEL_LANES = LANES // N_HEADS
V_ROWS = HEAD_DIM + 16
HALO = 32
CONV_ROWS = BLOCK + HALO
CONV_CHUNK = 32
ATTN_UNROLL = 4
OUT_TILE = 512
VMEM_LIMIT = 48 * 1024 * 1024


def _split_bf16(a):
    hi = a.astype(BF16)
    lo = (a - hi.astype(F32)).astype(BF16)
    return hi, lo


def _dot(a, b):
    return jnp.dot(a, b, preferred_element_type=F32)


def _dot3(a_hi, a_lo, b_hi, b_lo):
    return _dot(a_hi, b_hi) + (_dot(a_hi, b_lo) + _dot(a_lo, b_hi))


def _sigmoid(a):
    return 1.0 / (1.0 + jnp.exp(-a))


def _silu(a):
    return a * _sigmoid(a)


def _adaln_body(c_ref, w_ref, b_ref, o_ref):
    a_hi, a_lo = _split_bf16(_silu(c_ref[...]))
    w_hi, w_lo = _split_bf16(w_ref[...])
    o_ref[...] = _dot3(a_hi, a_lo, w_hi, w_lo) + b_ref[...]


def _adaln(c, w_ada, b_ada):
    bsz = c.shape[0]
    n_out = w_ada.shape[1]
    tn = 1024
    return pl.pallas_call(
        _adaln_body,
        grid=(n_out // tn,),
        in_specs=[pl.BlockSpec((bsz, D_MODEL), lambda j: (0, 0)),
                  pl.BlockSpec((D_MODEL, tn), lambda j: (0, j)),
                  pl.BlockSpec((1, tn), lambda j: (0, j))],
        out_specs=pl.BlockSpec((bsz, tn), lambda j: (0, j)),
        out_shape=jax.ShapeDtypeStruct((bsz, n_out), F32),
        compiler_params=pltpu.CompilerParams(dimension_semantics=("arbitrary",),
                                             vmem_limit_bytes=VMEM_LIMIT),
        name="adaln",
    )(c, w_ada, b_ada.reshape(1, n_out))


def _rope_table_body(inv_ref, pos_ref, cos_ref, sin_ref):
    pos = pos_ref[...].astype(F32)
    for i in range(ROPE_HALF):
        ang = pos * inv_ref[i]
        cos_ref[i] = jnp.cos(ang)
        sin_ref[i] = jnp.sin(ang)


def _rope_tables(positions):
    bsz, seq = positions.shape
    inv = ROPE_THETA ** (-(jnp.arange(ROPE_HALF, dtype=F32) * 2.0) / ROPE_DIM)
    cos, sin = pl.pallas_call(
        _rope_table_body,
        in_specs=[pl.BlockSpec(memory_space=pltpu.SMEM),
                  pl.BlockSpec((bsz, seq), lambda: (0, 0))],
        out_specs=[pl.BlockSpec((ROPE_HALF, bsz, seq), lambda: (0, 0, 0))] * 2,
        out_shape=[jax.ShapeDtypeStruct((ROPE_HALF, bsz, seq), F32)] * 2,
        name="rope_table",
    )(inv, positions)
    return cos.reshape(ROPE_HALF, bsz * seq), sin.reshape(ROPE_HALF, bsz * seq)


def _rope_t(t, cos, sin):
    pieces = []
    for hd in range(N_HEADS):
        r0 = hd * HEAD_DIM
        t1 = t[r0:r0 + ROPE_HALF]
        t2 = t[r0 + ROPE_HALF:r0 + ROPE_DIM]
        pieces += [t1 * cos - t2 * sin, t2 * cos + t1 * sin, t[r0 + ROPE_DIM:r0 + HEAD_DIM]]
    return jnp.concatenate(pieces, axis=0)


def _conv_steps(u, win_ref, conv_ref, wdw_ref):
    def setup():
        win_ref[0, HALO:, :] = u
        for s in range(1, SUBLANES):
            win_ref[s, 0:CONV_ROWS - SUBLANES, :] = win_ref[0, s:s + CONV_ROWS - SUBLANES, :]

    def chunk(c, after):
        r0 = c * CONV_CHUNK
        base = HALO - (CONV_WIDTH - 1)
        zero = jnp.minimum(jnp.maximum(after[0:CONV_CHUNK, :], 0.0), 0.0)
        acc = jnp.concatenate([zero] * (D_CONV // zero.shape[1]), axis=1)
        for w in range(CONV_WIDTH):
            a, s = divmod(base + w, SUBLANES)
            rows = slice(r0 + a * SUBLANES, r0 + a * SUBLANES + CONV_CHUNK)
            tap = jnp.concatenate([wdw_ref[w]] * (CONV_CHUNK // SUBLANES), axis=0)
            acc = acc + win_ref[s, rows, :] * tap
        conv_ref[r0:r0 + CONV_CHUNK, :] = acc

    return [lambda after: setup()] + [lambda after, c=c: chunk(c, after) for c in range(BLOCK // CONV_CHUNK)]


def _conv_finish(gate_act, win_ref, conv_ref, bdw_ref, gln_ref, bln_ref, wpw_ref, bpw_ref):
    win_ref[0, 0:HALO, :] = win_ref[0, BLOCK:BLOCK + HALO, :]
    y = conv_ref[...] + bdw_ref[...]
    mu = jnp.mean(y, axis=-1, keepdims=True)
    yc = y - mu
    var = jnp.mean(yc * yc, axis=-1, keepdims=True)
    z = _silu(yc * lax.rsqrt(var + EPS) * gln_ref[...] + bln_ref[...])
    return (_dot(z.astype(BF16), wpw_ref[...]) + bpw_ref[...]) * gate_act


def _inproj_body(x_ref, mod_ref, g_ref, cos_ref, sin_ref, whi_ref,
                 wdw_ref, bdw_ref, gln_ref, bln_ref, wpw_ref, bpw_ref,
                 yc_ref, qt_ref, k_ref, vt_ref, sga_ref, bias_ref,
                 km_ref, win_ref, conv_ref):
    i = pl.program_id(1)

    @pl.when(i == 0)
    def _():
        km_ref[...] = jnp.zeros_like(km_ref)
        win_ref[0, 0:HALO, :] = jnp.zeros((HALO, D_CONV), F32)

    x = x_ref[0]
    r = lax.rsqrt(jnp.mean(x * x, axis=-1, keepdims=True) + EPS)
    shift = mod_ref[0, 0:1, :]
    scale = mod_ref[0, 1:2, :]
    h = (x * r) * (g_ref[...] * (1.0 + scale)) + shift
    h_hi = h.astype(BF16)

    def w_hi(c0):
        return whi_ref[:, c0:c0 + 512]

    u = _dot(h_hi, w_hi(C_VAL)) * _sigmoid(_dot(h_hi, w_hi(C_GLU)))
    plan = [(lambda: _dot(h_hi, w_hi(C_GATE)), 1), (lambda: _dot(h_hi, w_hi(C_V)), 2),
            (lambda: _dot(h_hi, w_hi(C_AG)), 2), (lambda: _dot(h_hi, w_hi(C_Q)), 2),
            (lambda: _dot(h_hi, w_hi(C_K)), 2)]
    conv_steps = _conv_steps(u, win_ref, conv_ref, wdw_ref)
    assert sum(n for _, n in plan) == len(conv_steps)
    products = []
    for matmul, n_steps in plan:
        products.append(matmul())
        for _ in range(n_steps):
            conv_steps.pop(0)(products[-1])
    c_gate, v, a_gate, q, k = products
    y_conv = _conv_finish(_silu(c_gate), win_ref, conv_ref, bdw_ref, gln_ref, bln_ref, wpw_ref, bpw_ref)
    yc_ref[...] = y_conv.astype(BF16)

    vt = v.T.astype(BF16)
    ones = jnp.ones((V_ROWS - HEAD_DIM, BLOCK), BF16)
    vt_ref[...] = jnp.concatenate(
        [piece for hd in range(N_HEADS) for piece in (vt[hd * HEAD_DIM:(hd + 1) * HEAD_DIM], ones)], axis=0)
    sga_ref[...] = _silu(a_gate).astype(BF16)

    cos = cos_ref[...]
    sin = sin_ref[...]
    qt = _rope_t(q.T, cos, sin) * Q_SCALE
    kr = _rope_t(k.T, cos, sin).T
    qt_hi = qt.astype(BF16)
    qt_ref[...] = qt_hi
    k_ref[...] = kr.astype(BF16)

    km = km_ref[...]
    lane_k = lax.broadcasted_iota(jnp.int32, km.shape, 1)
    kmt = jnp.concatenate([jnp.where((lane_k // HEAD_DIM) == hd, km, 0.0) for hd in range(N_HEADS)], axis=0)
    gt = _dot(kmt.astype(BF16), qt_hi)

    nb = km.shape[0]
    n_iota = lax.broadcasted_iota(jnp.int32, (nb, BLOCK), 0)
    past = n_iota < i
    n_sel = jnp.minimum(i, TOP_K)
    biases = []
    for hd in range(N_HEADS):
        g = jnp.where(past, gt[hd * nb:(hd + 1) * nb, :], NEG)
        rank = jnp.zeros((nb, BLOCK), jnp.int32)
        for n2 in range(nb):
            row = g[n2:n2 + 1, :]
            beats = (row > g) | ((row == g) & (n_iota > n2))
            rank = rank + jnp.where(beats, 1, 0)
        biases.append(jnp.where(rank < n_sel, 0.0, NEG))
    bias_ref[...] = jnp.concatenate(biases, axis=0).astype(BF16)

    km_ref[pl.ds(i, 1), :] = jnp.mean(kr, axis=0, keepdims=True)


def _inproj(x, mod3, g_norm, cos_t, sin_t, w_hi, w_dw, b_dw, g_ln, b_ln, w_pw_bf, b_pw):
    bsz, seq, _ = x.shape
    nblk = seq // BLOCK
    n = bsz * seq
    row = lambda b, i: (b * nblk + i, 0)
    col = lambda b, i: (0, b * nblk + i)
    const = lambda b, i: (0, 0)
    rows_out = pl.BlockSpec((BLOCK, 512), row)
    rows_sd = jax.ShapeDtypeStruct((n, 512), BF16)
    vec = pl.BlockSpec((1, D_CONV), const)
    return pl.pallas_call(
        _inproj_body,
        grid=(bsz, nblk),
        in_specs=[pl.BlockSpec((1, BLOCK, D_MODEL), lambda b, i: (b, i, 0)),
                  pl.BlockSpec((1, 3, D_MODEL), lambda b, i: (b, 0, 0)),
                  pl.BlockSpec((1, D_MODEL), const),
                  pl.BlockSpec((ROPE_HALF, BLOCK), col),
                  pl.BlockSpec((ROPE_HALF, BLOCK), col),
                  pl.BlockSpec((D_MODEL, D_IN), const),
                  pl.BlockSpec((CONV_WIDTH, SUBLANES, D_CONV), lambda b, i: (0, 0, 0)),
                  vec, vec, vec,
                  pl.BlockSpec((D_CONV, D_CONV), const),
                  vec],
        out_specs=[rows_out,
                   pl.BlockSpec((D_ATTN, BLOCK), col),
                   rows_out,
                   pl.BlockSpec((N_HEADS * V_ROWS, BLOCK), col),
                   rows_out,
                   pl.BlockSpec((LANES, BLOCK), col)],
        out_shape=[rows_sd,
                   jax.ShapeDtypeStruct((D_ATTN, n), BF16),
                   rows_sd,
                   jax.ShapeDtypeStruct((N_HEADS * V_ROWS, n), BF16),
                   rows_sd,
                   jax.ShapeDtypeStruct((LANES, n), BF16)],
        scratch_shapes=[pltpu.VMEM((nblk, D_ATTN), F32),
                        pltpu.VMEM((SUBLANES, CONV_ROWS, D_CONV), F32),
                        pltpu.VMEM((BLOCK, D_CONV), F32)],
        compiler_params=pltpu.CompilerParams(dimension_semantics=("arbitrary", "arbitrary"),
                                             vmem_limit_bytes=VMEM_LIMIT),
        name="inproj",
    )(x, mod3, g_norm, cos_t, sin_t, w_hi,
      jnp.broadcast_to(w_dw[:, None, :], (CONV_WIDTH, SUBLANES, D_CONV)),
      b_dw.reshape(1, -1), g_ln.reshape(1, -1), b_ln.reshape(1, -1), w_pw_bf, b_pw.reshape(1, -1))


def _attn_body(qt_ref, bias_ref, k_ref, vt_ref, sga_ref, onehot_ref, o_ref,
               qa_ref, s_ref, m_ref, l_ref, acc_ref):
    i = pl.program_id(1)
    zeros_q = jnp.zeros((HEAD_DIM, BLOCK), BF16)
    for h in range(N_HEADS):
        qh = qt_ref[h * HEAD_DIM:(h + 1) * HEAD_DIM, :]
        bh = bias_ref[h * SEL_LANES:(h + 1) * SEL_LANES, :]
        pieces = [qh, zeros_q] if h % 2 == 0 else [zeros_q, qh]
        if h > 0:
            pieces.append(jnp.zeros((h * SEL_LANES, BLOCK), BF16))
        pieces.append(bh)
        if h < N_HEADS - 1:
            pieces.append(jnp.zeros(((N_HEADS - 1 - h) * SEL_LANES, BLOCK), BF16))
        qa_ref[h] = jnp.concatenate(pieces, axis=0)

    def score_fn(j, own):
        start = pl.multiple_of(j * BLOCK, BLOCK)
        sel = jnp.zeros((BLOCK, LANES), BF16) if own else onehot_ref[j]
        operands = {}

        def f(h):
            pair = h // 2
            if pair not in operands:
                kj = k_ref[pl.ds(start, BLOCK), pair * LANES:(pair + 1) * LANES]
                operands[pair] = jnp.concatenate([kj, sel], axis=1)
            return _dot(operands[pair], qa_ref[h])
        return f

    def head_step(j, h, st, first):
        start = pl.multiple_of(j * BLOCK, BLOCK)
        m_cur = jnp.max(st, axis=0, keepdims=True)
        if first:
            m_new = m_cur
        else:
            m_prev = m_ref[h:h + 1, :]
            m_new = jnp.maximum(m_prev, m_cur)
            alpha = jnp.exp2(m_prev - m_new)
        pt = jnp.exp2(st - m_new)
        vth = vt_ref[h * V_ROWS:(h + 1) * V_ROWS, pl.ds(start, BLOCK)]
        pv_l = _dot(vth, pt.astype(BF16))
        pv = pv_l[0:HEAD_DIM]
        l_cur = pv_l[HEAD_DIM:HEAD_DIM + 1]
        rows = slice(h * HEAD_DIM, (h + 1) * HEAD_DIM)
        acc_ref[rows, :] = pv if first else alpha * acc_ref[rows, :] + pv
        l_ref[h:h + 1, :] = l_cur if first else alpha * l_ref[h:h + 1, :] + l_cur
        m_ref[h:h + 1, :] = m_new

    key_idx = lax.broadcasted_iota(jnp.int32, (BLOCK, BLOCK), 0)
    qry_idx = lax.broadcasted_iota(jnp.int32, (BLOCK, BLOCK), 1)
    causal = key_idx <= qry_idx
    own_fn = score_fn(i, True)
    own = [own_fn(h) for h in range(N_HEADS)]

    def overlapped(slot_next, j_next, j_cur, cur, first):
        nxt = score_fn(j_next, False)
        for h in range(N_HEADS):
            s_ref[slot_next, h] = nxt(h)
            head_step(j_cur, h, cur(h), first)

    odd = i % 2
    overlapped(odd, 0, i, lambda h: jnp.where(causal, own[h], NEG), True)

    @pl.when(odd == 1)
    def _():
        overlapped(0, 1, 0, lambda h: s_ref[1, h], False)

    def run_blocks(j0, count):
        for d in range(0, count, 2):
            overlapped(1, j0 + d + 1, j0 + d, lambda h: s_ref[0, h], False)
            overlapped(0, jnp.minimum(j0 + d + 2, i), j0 + d + 1, lambda h: s_ref[1, h], False)

    pairs = i // 2

    @pl.when(pairs % 2 == 1)
    def _():
        run_blocks(odd, 2)

    def quad_body(t, carry):
        run_blocks(odd + 2 * (pairs % 2) + ATTN_UNROLL * t, ATTN_UNROLL)
        return carry

    lax.fori_loop(0, pairs // 2, quad_body, 0)

    outs = [acc_ref[h * HEAD_DIM:(h + 1) * HEAD_DIM, :] * (1.0 / l_ref[h:h + 1, :]) for h in range(N_HEADS)]
    out = jnp.concatenate(outs, axis=0).T
    o_ref[...] = (out * sga_ref[...].astype(F32)).astype(BF16)


def _attention(qt, bias_t, k, vt, sga, bsz, seq):
    n = bsz * seq
    nblk = seq // BLOCK
    assert nblk == SEL_LANES, "one selection lane per key block"
    row = lambda b, i: (b * nblk + i, 0)
    col = lambda b, i: (0, b * nblk + i)
    lane = lax.broadcasted_iota(jnp.int32, (nblk, BLOCK, LANES), 2)
    blk = lax.broadcasted_iota(jnp.int32, (nblk, BLOCK, LANES), 0)
    onehot = ((lane % SEL_LANES) == blk).astype(BF16)
    return pl.pallas_call(
        _attn_body,
        grid=(bsz, nblk),
        in_specs=[pl.BlockSpec((D_ATTN, BLOCK), col),
                  pl.BlockSpec((LANES, BLOCK), col),
                  pl.BlockSpec((seq, D_ATTN), lambda b, i: (b, 0)),
                  pl.BlockSpec((N_HEADS * V_ROWS, seq), lambda b, i: (0, b)),
                  pl.BlockSpec((BLOCK, D_ATTN), row),
                  pl.BlockSpec((nblk, BLOCK, LANES), lambda b, i: (0, 0, 0))],
        out_specs=pl.BlockSpec((BLOCK, D_ATTN), row),
        out_shape=jax.ShapeDtypeStruct((n, D_ATTN), BF16),
        scratch_shapes=[pltpu.VMEM((N_HEADS, 2 * LANES, BLOCK), BF16),
                        pltpu.VMEM((2, N_HEADS, BLOCK, BLOCK), F32),
                        pltpu.VMEM((N_HEADS, BLOCK), F32),
                        pltpu.VMEM((N_HEADS, BLOCK), F32),
                        pltpu.VMEM((D_ATTN, BLOCK), F32)],
        compiler_params=pltpu.CompilerParams(dimension_semantics=("arbitrary", "arbitrary"),
                                             vmem_limit_bytes=VMEM_LIMIT),
        name="attention",
    )(qt, bias_t, k, vt, sga, onehot)


def _outproj_body(x_ref, mod_ref, yc_ref, ya_ref, w_ref, gf_ref, o_ref):
    y = _dot(yc_ref[...], w_ref[0:D_CONV, :]) + _dot(ya_ref[...], w_ref[D_CONV:, :])
    gate = mod_ref[0, 2:3, :]
    z = x_ref[0] + gate * y
    r = lax.rsqrt(jnp.mean(z * z, axis=-1, keepdims=True) + EPS)
    o_ref[0] = z * r * gf_ref[...]


def _outproj(x, mod3, y_conv, y_attn, w_out_bf, g_final):
    bsz, seq, _ = x.shape
    nt = seq // OUT_TILE
    row = lambda b, t: (b * nt + t, 0)
    const = lambda b, t: (0, 0)
    xspec = pl.BlockSpec((1, OUT_TILE, D_MODEL), lambda b, t: (b, t, 0))
    return pl.pallas_call(
        _outproj_body,
        grid=(bsz, nt),
        in_specs=[xspec,
                  pl.BlockSpec((1, 3, D_MODEL), lambda b, t: (b, 0, 0)),
                  pl.BlockSpec((OUT_TILE, D_CONV), row),
                  pl.BlockSpec((OUT_TILE, D_ATTN), row),
                  pl.BlockSpec((D_MODEL, D_MODEL), const),
                  pl.BlockSpec((1, D_MODEL), const)],
        out_specs=xspec,
        out_shape=jax.ShapeDtypeStruct(x.shape, F32),
        compiler_params=pltpu.CompilerParams(dimension_semantics=("arbitrary", "arbitrary"),
                                             vmem_limit_bytes=VMEM_LIMIT),
        name="outproj",
    )(x, mod3, y_conv, y_attn, w_out_bf, g_final.reshape(1, -1))


def kernel(x, c, positions, w_ada, b_ada, g_norm, w_in, w_dw, b_dw, g_ln_conv, b_ln_conv, w_pw, b_pw,
           w_out, g_final):
    bsz, seq, _ = x.shape
    assert w_ada.shape[0] == 1, "single-layer block: the final rmsnorm is fused into the out-projection"
    cos_t, sin_t = _rope_tables(positions)
    mod3 = _adaln(c, w_ada[0], b_ada[0]).reshape(bsz, 3, D_MODEL)
    w_hi = w_in[0].astype(BF16)
    y_conv, qt, k, vt, sga, bias_t = _inproj(x, mod3, g_norm[0].reshape(1, -1), cos_t, sin_t, w_hi,
                                             w_dw[0], b_dw[0], g_ln_conv[0], b_ln_conv[0],
                                             w_pw[0].astype(BF16), b_pw[0])
    y_attn = _attention(qt, bias_t, k, vt, sga, bsz, seq)
    return _outproj(x, mod3, y_conv, y_attn, w_out[0].astype(BF16), g_final)
```

```python
import jax
import jax.numpy as jnp
from jax import lax
from jax.experimental import pallas as pl
from jax.experimental.pallas import tpu as pltpu

F32 = jnp.float32
BF16 = jnp.bfloat16

D_MODEL = 1024
D_CONV = 512
D_ATTN = 512
HEAD_DIM = 64
N_HEADS = 8
CONV_WIDTH = 31
BLOCK = 256
TOP_K = 3
ROPE_DIM = 16
ROPE_HALF = 8
ROPE_THETA = 500000.0
EPS = 1e-6
NEG = -1e30
Q_SCALE = HEAD_DIM ** -0.5 * 1.4426950408889634
D_IN = 3584
C_VAL, C_GLU, C_GATE, C_Q, C_K, C_V, C_AG = 0, 512, 1024, 1536, 2048, 2560, 3072

LANES = 128
SUBLANES = 8
SEL_LANES = LANES // N_HEADS
V_ROWS = HEAD_DIM + 16
HALO = 32
CONV_ROWS = BLOCK + HALO
CONV_CHUNK = 32
ATTN_UNROLL = 4
VMEM_LIMIT = 48 * 1024 * 1024


def _split_bf16(a):
    hi = a.astype(BF16)
    lo = (a - hi.astype(F32)).astype(BF16)
    return hi, lo


def _dot(a, b):
    return jnp.dot(a, b, preferred_element_type=F32)


def _dot3(a_hi, a_lo, b_hi, b_lo):
    return _dot(a_hi, b_hi) + (_dot(a_hi, b_lo) + _dot(a_lo, b_hi))


def _sigmoid(a):
    return 1.0 / (1.0 + jnp.exp(-a))


def _silu(a):
    return a * _sigmoid(a)


def _adaln_body(c_ref, w_ref, b_ref, o_ref):
    a_hi, a_lo = _split_bf16(_silu(c_ref[...]))
    w_hi, w_lo = _split_bf16(w_ref[...])
    o_ref[...] = _dot3(a_hi, a_lo, w_hi, w_lo) + b_ref[...]


def _adaln(c, w_ada, b_ada):
    bsz = c.shape[0]
    n_out = w_ada.shape[1]
    tn = 1024
    return pl.pallas_call(
        _adaln_body,
        grid=(n_out // tn,),
        in_specs=[pl.BlockSpec((bsz, D_MODEL), lambda j: (0, 0)),
                  pl.BlockSpec((D_MODEL, tn), lambda j: (0, j)),
                  pl.BlockSpec((1, tn), lambda j: (0, j))],
        out_specs=pl.BlockSpec((bsz, tn), lambda j: (0, j)),
        out_shape=jax.ShapeDtypeStruct((bsz, n_out), F32),
        compiler_params=pltpu.CompilerParams(dimension_semantics=("arbitrary",),
                                             vmem_limit_bytes=VMEM_LIMIT),
        name="adaln",
    )(c, w_ada, b_ada.reshape(1, n_out))


def _rope_table_body(inv_ref, pos_ref, cos_ref, sin_ref):
    pos = pos_ref[...].astype(F32)
    for i in range(ROPE_HALF):
        ang = pos * inv_ref[i]
        cos_ref[i] = jnp.cos(ang)
        sin_ref[i] = jnp.sin(ang)


def _rope_tables(positions):
    bsz, seq = positions.shape
    inv = ROPE_THETA ** (-(jnp.arange(ROPE_HALF, dtype=F32) * 2.0) / ROPE_DIM)
    cos, sin = pl.pallas_call(
        _rope_table_body,
        in_specs=[pl.BlockSpec(memory_space=pltpu.SMEM),
                  pl.BlockSpec((bsz, seq), lambda: (0, 0))],
        out_specs=[pl.BlockSpec((ROPE_HALF, bsz, seq), lambda: (0, 0, 0))] * 2,
        out_shape=[jax.ShapeDtypeStruct((ROPE_HALF, bsz, seq), F32)] * 2,
        name="rope_table",
    )(inv, positions)
    return cos.reshape(ROPE_HALF, bsz * seq), sin.reshape(ROPE_HALF, bsz * seq)


def _rope_t(t, cos, sin):
    pieces = []
    for hd in range(N_HEADS):
        r0 = hd * HEAD_DIM
        t1 = t[r0:r0 + ROPE_HALF]
        t2 = t[r0 + ROPE_HALF:r0 + ROPE_DIM]
        pieces += [t1 * cos - t2 * sin, t2 * cos + t1 * sin, t[r0 + ROPE_DIM:r0 + HEAD_DIM]]
    return jnp.concatenate(pieces, axis=0)


def _conv_steps(u, win_ref, conv_ref, wdw_ref):
    def setup():
        win_ref[0, HALO:, :] = u
        for s in range(1, SUBLANES):
            win_ref[s, 0:CONV_ROWS - SUBLANES, :] = win_ref[0, s:s + CONV_ROWS - SUBLANES, :]

    def chunk(c, after):
        r0 = c * CONV_CHUNK
        base = HALO - (CONV_WIDTH - 1)
        zero = jnp.minimum(jnp.maximum(after[0:CONV_CHUNK, :], 0.0), 0.0)
        acc = jnp.concatenate([zero] * (D_CONV // zero.shape[1]), axis=1)
        for w in range(CONV_WIDTH):
            a, s = divmod(base + w, SUBLANES)
            rows = slice(r0 + a * SUBLANES, r0 + a * SUBLANES + CONV_CHUNK)
            tap = jnp.concatenate([wdw_ref[w]] * (CONV_CHUNK // SUBLANES), axis=0)
            acc = acc + win_ref[s, rows, :] * tap
        conv_ref[r0:r0 + CONV_CHUNK, :] = acc

    return [lambda after: setup()] + [lambda after, c=c: chunk(c, after) for c in range(BLOCK // CONV_CHUNK)]


def _conv_finish(gate_act, win_ref, conv_ref, bdw_ref, gln_ref, bln_ref, wpw_ref, bpw_ref):
    win_ref[0, 0:HALO, :] = win_ref[0, BLOCK:BLOCK + HALO, :]
    y = conv_ref[...] + bdw_ref[...]
    mu = jnp.mean(y, axis=-1, keepdims=True)
    yc = y - mu
    var = jnp.mean(yc * yc, axis=-1, keepdims=True)
    z = _silu(yc * lax.rsqrt(var + EPS) * gln_ref[...] + bln_ref[...])
    return (_dot(z.astype(BF16), wpw_ref[...]) + bpw_ref[...]) * gate_act


def _inproj_body(x_ref, mod_ref, g_ref, cos_ref, sin_ref, whi_ref,
                 wdw_ref, bdw_ref, gln_ref, bln_ref, wpw_ref, bpw_ref,
                 yc_ref, qt_ref, k_ref, vt_ref, sga_ref, bias_ref,
                 km_ref, win_ref, conv_ref):
    i = pl.program_id(1)

    @pl.when(i == 0)
    def _():
        km_ref[...] = jnp.zeros_like(km_ref)
        win_ref[0, 0:HALO, :] = jnp.zeros((HALO, D_CONV), F32)

    x = x_ref[0]
    r = lax.rsqrt(jnp.mean(x * x, axis=-1, keepdims=True) + EPS)
    shift = mod_ref[0, 0:1, :]
    scale = mod_ref[0, 1:2, :]
    h = (x * r) * (g_ref[...] * (1.0 + scale)) + shift
    h_hi = h.astype(BF16)

    def w_hi(c0):
        return whi_ref[:, c0:c0 + 512]

    u = _dot(h_hi, w_hi(C_VAL)) * _sigmoid(_dot(h_hi, w_hi(C_GLU)))
    plan = [(lambda: _dot(h_hi, w_hi(C_GATE)), 1), (lambda: _dot(h_hi, w_hi(C_V)), 2),
            (lambda: _dot(h_hi, w_hi(C_AG)), 2), (lambda: _dot(h_hi, w_hi(C_Q)), 2),
            (lambda: _dot(h_hi, w_hi(C_K)), 2)]
    conv_steps = _conv_steps(u, win_ref, conv_ref, wdw_ref)
    assert sum(n for _, n in plan) == len(conv_steps)
    products = []
    for matmul, n_steps in plan:
        products.append(matmul())
        for _ in range(n_steps):
            conv_steps.pop(0)(products[-1])
    c_gate, v, a_gate, q, k = products
    y_conv = _conv_finish(_silu(c_gate), win_ref, conv_ref, bdw_ref, gln_ref, bln_ref, wpw_ref, bpw_ref)
    yc_ref[...] = y_conv.astype(BF16)

    vt = v.T.astype(BF16)
    ones = jnp.ones((V_ROWS - HEAD_DIM, BLOCK), BF16)
    vt_ref[...] = jnp.concatenate(
        [piece for hd in range(N_HEADS) for piece in (vt[hd * HEAD_DIM:(hd + 1) * HEAD_DIM], ones)], axis=0)
    sga_ref[...] = _silu(a_gate).astype(BF16)

    cos = cos_ref[...]
    sin = sin_ref[...]
    qt = _rope_t(q.T, cos, sin) * Q_SCALE
    kr = _rope_t(k.T, cos, sin).T
    qt_hi = qt.astype(BF16)
    qt_ref[...] = qt_hi
    k_ref[...] = kr.astype(BF16)

    km = km_ref[...]
    lane_k = lax.broadcasted_iota(jnp.int32, km.shape, 1)
    kmt = jnp.concatenate([jnp.where((lane_k // HEAD_DIM) == hd, km, 0.0) for hd in range(N_HEADS)], axis=0)
    gt = _dot(kmt.astype(BF16), qt_hi)

    nb = km.shape[0]
    n_iota = lax.broadcasted_iota(jnp.int32, (nb, BLOCK), 0)
    past = n_iota < i
    n_sel = jnp.minimum(i, TOP_K)
    biases = []
    for hd in range(N_HEADS):
        g = jnp.where(past, gt[hd * nb:(hd + 1) * nb, :], NEG)
        rank = jnp.zeros((nb, BLOCK), jnp.int32)
        for n2 in range(nb):
            row = g[n2:n2 + 1, :]
            beats = (row > g) | ((row == g) & (n_iota > n2))
            rank = rank + jnp.where(beats, 1, 0)
        biases.append(jnp.where(rank < n_sel, 0.0, NEG))
    bias_ref[...] = jnp.concatenate(biases, axis=0).astype(BF16)

    km_ref[pl.ds(i, 1), :] = jnp.mean(kr, axis=0, keepdims=True)


def _inproj(x, mod3, g_norm, cos_t, sin_t, w_hi, w_dw, b_dw, g_ln, b_ln, w_pw_bf, b_pw):
    bsz, seq, _ = x.shape
    nblk = seq // BLOCK
    n = bsz * seq
    row = lambda b, i: (b * nblk + i, 0)
    col = lambda b, i: (0, b * nblk + i)
    const = lambda b, i: (0, 0)
    rows_out = pl.BlockSpec((BLOCK, 512), row)
    rows_sd = jax.ShapeDtypeStruct((n, 512), BF16)
    vec = pl.BlockSpec((1, D_CONV), const)
    return pl.pallas_call(
        _inproj_body,
        grid=(bsz, nblk),
        in_specs=[pl.BlockSpec((1, BLOCK, D_MODEL), lambda b, i: (b, i, 0)),
                  pl.BlockSpec((1, 3, D_MODEL), lambda b, i: (b, 0, 0)),
                  pl.BlockSpec((1, D_MODEL), const),
                  pl.BlockSpec((ROPE_HALF, BLOCK), col),
                  pl.BlockSpec((ROPE_HALF, BLOCK), col),
                  pl.BlockSpec((D_MODEL, D_IN), const),
                  pl.BlockSpec((CONV_WIDTH, SUBLANES, D_CONV), lambda b, i: (0, 0, 0)),
                  vec, vec, vec,
                  pl.BlockSpec((D_CONV, D_CONV), const),
                  vec],
        out_specs=[rows_out,
                   pl.BlockSpec((D_ATTN, BLOCK), col),
                   rows_out,
                   pl.BlockSpec((N_HEADS * V_ROWS, BLOCK), col),
                   rows_out,
                   pl.BlockSpec((LANES, BLOCK), col)],
        out_shape=[rows_sd,
                   jax.ShapeDtypeStruct((D_ATTN, n), BF16),
                   rows_sd,
                   jax.ShapeDtypeStruct((N_HEADS * V_ROWS, n), BF16),
                   rows_sd,
                   jax.ShapeDtypeStruct((LANES, n), BF16)],
        scratch_shapes=[pltpu.VMEM((nblk, D_ATTN), F32),
                        pltpu.VMEM((SUBLANES, CONV_ROWS, D_CONV), F32),
                        pltpu.VMEM((BLOCK, D_CONV), F32)],
        compiler_params=pltpu.CompilerParams(dimension_semantics=("arbitrary", "arbitrary"),
                                             vmem_limit_bytes=VMEM_LIMIT),
        name="inproj",
    )(x, mod3, g_norm, cos_t, sin_t, w_hi,
      jnp.broadcast_to(w_dw[:, None, :], (CONV_WIDTH, SUBLANES, D_CONV)),
      b_dw.reshape(1, -1), g_ln.reshape(1, -1), b_ln.reshape(1, -1), w_pw_bf, b_pw.reshape(1, -1))


def _attn_body(qt_ref, bias_ref, k_ref, vt_ref, sga_ref, onehot_ref, x_ref, mod_ref, yc_ref, wout_ref, gf_ref,
               o_ref, qa_ref, s_ref, smax_ref, m_ref, l_ref, acc_ref):
    i = pl.program_id(1)
    y_out_conv = _dot(yc_ref[...], wout_ref[0:D_CONV, :])
    zeros_q = jnp.zeros((HEAD_DIM, BLOCK), BF16)
    for h in range(N_HEADS):
        qh = qt_ref[h * HEAD_DIM:(h + 1) * HEAD_DIM, :]
        bh = bias_ref[h * SEL_LANES:(h + 1) * SEL_LANES, :]
        pieces = [qh, zeros_q] if h % 2 == 0 else [zeros_q, qh]
        if h > 0:
            pieces.append(jnp.zeros((h * SEL_LANES, BLOCK), BF16))
        pieces.append(bh)
        if h < N_HEADS - 1:
            pieces.append(jnp.zeros(((N_HEADS - 1 - h) * SEL_LANES, BLOCK), BF16))
        qa_ref[h] = jnp.concatenate(pieces, axis=0)

    def score_fn(j, own):
        start = pl.multiple_of(j * BLOCK, BLOCK)
        sel = jnp.zeros((BLOCK, LANES), BF16) if own else onehot_ref[j]
        operands = {}

        def f(h):
            pair = h // 2
            if pair not in operands:
                kj = k_ref[pl.ds(start, BLOCK), pair * LANES:(pair + 1) * LANES]
                operands[pair] = jnp.concatenate([kj, sel], axis=1)
            return _dot(operands[pair], qa_ref[h])
        return f

    def head_step(j, h, st, m_cur, first):
        start = pl.multiple_of(j * BLOCK, BLOCK)
        if first:
            m_new = m_cur
        else:
            m_prev = m_ref[h:h + 1, :]
            m_new = jnp.maximum(m_prev, m_cur)
            alpha = jnp.exp2(m_prev - m_new)
        pt = jnp.exp2(st - m_new)
        vth = vt_ref[h * V_ROWS:(h + 1) * V_ROWS, pl.ds(start, BLOCK)]
        pv_l = _dot(vth, pt.astype(BF16))
        pv = pv_l[0:HEAD_DIM]
        l_cur = pv_l[HEAD_DIM:HEAD_DIM + 1]
        rows = slice(h * HEAD_DIM, (h + 1) * HEAD_DIM)
        acc_ref[rows, :] = pv if first else alpha * acc_ref[rows, :] + pv
        l_ref[h:h + 1, :] = l_cur if first else alpha * l_ref[h:h + 1, :] + l_cur
        m_ref[h:h + 1, :] = m_new

    key_idx = lax.broadcasted_iota(jnp.int32, (BLOCK, BLOCK), 0)
    qry_idx = lax.broadcasted_iota(jnp.int32, (BLOCK, BLOCK), 1)
    causal = key_idx <= qry_idx
    own_fn = score_fn(i, True)
    own = [own_fn(h) for h in range(N_HEADS)]

    def overlapped(slot_next, j_next, j_cur, cur, first):
        nxt = score_fn(j_next, False)
        for h in range(N_HEADS):
            st_next = nxt(h)
            s_ref[slot_next, h] = st_next
            smax_ref[slot_next, h:h + 1, :] = jnp.max(st_next, axis=0, keepdims=True)
            head_step(j_cur, h, *cur(h), first)

    odd = i % 2
    def own_scores(h):
        st = jnp.where(causal, own[h], NEG)
        return st, jnp.max(st, axis=0, keepdims=True)

    def stored(slot):
        return lambda h: (s_ref[slot, h], smax_ref[slot, h:h + 1, :])

    overlapped(odd, 0, i, own_scores, True)

    @pl.when(odd == 1)
    def _():
        overlapped(0, 1, 0, stored(1), False)

    def run_blocks(j0, count):
        for d in range(0, count, 2):
            overlapped(1, j0 + d + 1, j0 + d, stored(0), False)
            overlapped(0, jnp.minimum(j0 + d + 2, i), j0 + d + 1, stored(1), False)

    pairs = i // 2

    @pl.when(pairs % 2 == 1)
    def _():
        run_blocks(odd, 2)

    def quad_body(t, carry):
        run_blocks(odd + 2 * (pairs % 2) + ATTN_UNROLL * t, ATTN_UNROLL)
        return carry

    lax.fori_loop(0, pairs // 2, quad_body, 0)

    outs = [acc_ref[h * HEAD_DIM:(h + 1) * HEAD_DIM, :] * (1.0 / l_ref[h:h + 1, :]) for h in range(N_HEADS)]
    out = jnp.concatenate(outs, axis=0).T
    y_attn = (out * sga_ref[...].astype(F32)).astype(BF16)

    y = y_out_conv + _dot(y_attn, wout_ref[D_CONV:, :])
    z = x_ref[0] + mod_ref[0, 2:3, :] * y
    r = lax.rsqrt(jnp.mean(z * z, axis=-1, keepdims=True) + EPS)
    o_ref[0] = z * r * gf_ref[...]


def _attention_outproj(qt, bias_t, k, vt, sga, x, mod3, y_conv, w_out_bf, g_final):
    bsz, seq, _ = x.shape
    nblk = seq // BLOCK
    assert nblk == SEL_LANES, "one selection lane per key block"
    row = lambda b, i: (b * nblk + i, 0)
    col = lambda b, i: (0, b * nblk + i)
    const = lambda b, i: (0, 0)
    xspec = pl.BlockSpec((1, BLOCK, D_MODEL), lambda b, i: (b, i, 0))
    lane = lax.broadcasted_iota(jnp.int32, (nblk, BLOCK, LANES), 2)
    blk = lax.broadcasted_iota(jnp.int32, (nblk, BLOCK, LANES), 0)
    onehot = ((lane % SEL_LANES) == blk).astype(BF16)
    return pl.pallas_call(
        _attn_body,
        grid=(bsz, nblk),
        in_specs=[pl.BlockSpec((D_ATTN, BLOCK), col),
                  pl.BlockSpec((LANES, BLOCK), col),
                  pl.BlockSpec((seq, D_ATTN), lambda b, i: (b, 0)),
                  pl.BlockSpec((N_HEADS * V_ROWS, seq), lambda b, i: (0, b)),
                  pl.BlockSpec((BLOCK, D_ATTN), row),
                  pl.BlockSpec((nblk, BLOCK, LANES), lambda b, i: (0, 0, 0)),
                  xspec,
                  pl.BlockSpec((1, 3, D_MODEL), lambda b, i: (b, 0, 0)),
                  pl.BlockSpec((BLOCK, D_CONV), row),
                  pl.BlockSpec((D_MODEL, D_MODEL), const),
                  pl.BlockSpec((1, D_MODEL), const)],
        out_specs=xspec,
        out_shape=jax.ShapeDtypeStruct(x.shape, F32),
        scratch_shapes=[pltpu.VMEM((N_HEADS, 2 * LANES, BLOCK), BF16),
                        pltpu.VMEM((2, N_HEADS, BLOCK, BLOCK), F32),
                        pltpu.VMEM((2, N_HEADS, BLOCK), F32),
                        pltpu.VMEM((N_HEADS, BLOCK), F32),
                        pltpu.VMEM((N_HEADS, BLOCK), F32),
                        pltpu.VMEM((D_ATTN, BLOCK), F32)],
        compiler_params=pltpu.CompilerParams(dimension_semantics=("arbitrary", "arbitrary"),
                                             vmem_limit_bytes=VMEM_LIMIT),
        name="attention_outproj",
    )(qt, bias_t, k, vt, sga, onehot, x, mod3, y_conv, w_out_bf, g_final.reshape(1, -1))


def kernel(x, c, positions, w_ada, b_ada, g_norm, w_in, w_dw, b_dw, g_ln_conv, b_ln_conv, w_pw, b_pw,
           w_out, g_final):
    bsz, seq, _ = x.shape
    assert w_ada.shape[0] == 1, "single-layer block: the final rmsnorm is fused into the last kernel"
    cos_t, sin_t = _rope_tables(positions)
    mod3 = _adaln(c, w_ada[0], b_ada[0]).reshape(bsz, 3, D_MODEL)
    w_hi = w_in[0].astype(BF16)
    y_conv, qt, k, vt, sga, bias_t = _inproj(x, mod3, g_norm[0].reshape(1, -1), cos_t, sin_t, w_hi,
                                             w_dw[0], b_dw[0], g_ln_conv[0], b_ln_conv[0],
                                             w_pw[0].astype(BF16), b_pw[0])
    return _attention_outproj(qt, bias_t, k, vt, sga, x, mod3, y_conv, w_out[0].astype(BF16), g_final)
```

```python
import jax
import jax.numpy as jnp
from jax import lax
from jax.experimental import pallas as pl
from jax.experimental.pallas import tpu as pltpu

F32 = jnp.float32
BF16 = jnp.bfloat16

D_MODEL = 1024
D_CONV = 512
D_ATTN = 512
HEAD_DIM = 64
N_HEADS = 8
CONV_WIDTH = 31
BLOCK = 256
TOP_K = 3
ROPE_DIM = 16
ROPE_HALF = 8
ROPE_THETA = 500000.0
EPS = 1e-6
NEG = -1e30
Q_SCALE = HEAD_DIM ** -0.5 * 1.4426950408889634
D_IN = 3584
C_VAL, C_GLU, C_GATE, C_Q, C_K, C_V, C_AG = 0, 512, 1024, 1536, 2048, 2560, 3072

LANES = 128
SUBLANES = 8
SEL_LANES = LANES // N_HEADS
V_ROWS = HEAD_DIM + 16
HALO = 32
CONV_ROWS = BLOCK + HALO
CONV_CHUNK = 32
ATTN_UNROLL = 4
VMEM_LIMIT = 48 * 1024 * 1024


def _split_bf16(a):
    hi = a.astype(BF16)
    lo = (a - hi.astype(F32)).astype(BF16)
    return hi, lo


def _dot(a, b):
    return jnp.dot(a, b, preferred_element_type=F32)


def _dot3(a_hi, a_lo, b_hi, b_lo):
    return _dot(a_hi, b_hi) + (_dot(a_hi, b_lo) + _dot(a_lo, b_hi))


def _sigmoid(a):
    return 1.0 / (1.0 + jnp.exp(-a))


def _silu(a):
    return a * _sigmoid(a)


def _adaln_body(c_ref, w_ref, b_ref, o_ref):
    a_hi, a_lo = _split_bf16(_silu(c_ref[...]))
    w_hi, w_lo = _split_bf16(w_ref[...])
    o_ref[...] = _dot3(a_hi, a_lo, w_hi, w_lo) + b_ref[...]


def _adaln(c, w_ada, b_ada):
    bsz = c.shape[0]
    n_out = w_ada.shape[1]
    tn = 1024
    return pl.pallas_call(
        _adaln_body,
        grid=(n_out // tn,),
        in_specs=[pl.BlockSpec((bsz, D_MODEL), lambda j: (0, 0)),
                  pl.BlockSpec((D_MODEL, tn), lambda j: (0, j)),
                  pl.BlockSpec((1, tn), lambda j: (0, j))],
        out_specs=pl.BlockSpec((bsz, tn), lambda j: (0, j)),
        out_shape=jax.ShapeDtypeStruct((bsz, n_out), F32),
        compiler_params=pltpu.CompilerParams(dimension_semantics=("arbitrary",),
                                             vmem_limit_bytes=VMEM_LIMIT),
        name="adaln",
    )(c, w_ada, b_ada.reshape(1, n_out))


def _rope_table_body(inv_ref, pos_ref, cos_ref, sin_ref):
    pos = pos_ref[...].astype(F32)
    for i in range(ROPE_HALF):
        ang = pos * inv_ref[i]
        cos_ref[i] = jnp.cos(ang)
        sin_ref[i] = jnp.sin(ang)


def _rope_tables(positions):
    bsz, seq = positions.shape
    inv = ROPE_THETA ** (-(jnp.arange(ROPE_HALF, dtype=F32) * 2.0) / ROPE_DIM)
    cos, sin = pl.pallas_call(
        _rope_table_body,
        in_specs=[pl.BlockSpec(memory_space=pltpu.SMEM),
                  pl.BlockSpec((bsz, seq), lambda: (0, 0))],
        out_specs=[pl.BlockSpec((ROPE_HALF, bsz, seq), lambda: (0, 0, 0))] * 2,
        out_shape=[jax.ShapeDtypeStruct((ROPE_HALF, bsz, seq), F32)] * 2,
        name="rope_table",
    )(inv, positions)
    return cos.reshape(ROPE_HALF, bsz * seq), sin.reshape(ROPE_HALF, bsz * seq)


def _rope_t(t, cos, sin):
    pieces = []
    for hd in range(N_HEADS):
        r0 = hd * HEAD_DIM
        t1 = t[r0:r0 + ROPE_HALF]
        t2 = t[r0 + ROPE_HALF:r0 + ROPE_DIM]
        pieces += [t1 * cos - t2 * sin, t2 * cos + t1 * sin, t[r0 + ROPE_DIM:r0 + HEAD_DIM]]
    return jnp.concatenate(pieces, axis=0)


def _conv_steps(u, win_ref, conv_ref, wdw_ref):
    def setup():
        win_ref[0, HALO:, :] = u
        for s in range(1, SUBLANES):
            win_ref[s, 0:CONV_ROWS - SUBLANES, :] = win_ref[0, s:s + CONV_ROWS - SUBLANES, :]

    def chunk(c, after):
        r0 = c * CONV_CHUNK
        base = HALO - (CONV_WIDTH - 1)
        zero = jnp.minimum(jnp.maximum(after[0:CONV_CHUNK, :], 0.0), 0.0)
        acc = jnp.concatenate([zero] * (D_CONV // zero.shape[1]), axis=1)
        for w in range(CONV_WIDTH):
            a, s = divmod(base + w, SUBLANES)
            rows = slice(r0 + a * SUBLANES, r0 + a * SUBLANES + CONV_CHUNK)
            tap = jnp.concatenate([wdw_ref[w]] * (CONV_CHUNK // SUBLANES), axis=0)
            acc = acc + win_ref[s, rows, :] * tap
        conv_ref[r0:r0 + CONV_CHUNK, :] = acc

    return [lambda after: setup()] + [lambda after, c=c: chunk(c, after) for c in range(BLOCK // CONV_CHUNK)]


def _conv_finish(win_ref, conv_ref, bdw_ref, gln_ref, bln_ref, wpw_ref, bpw_ref):
    win_ref[0, 0:HALO, :] = win_ref[0, BLOCK:BLOCK + HALO, :]
    y = conv_ref[...] + bdw_ref[...]
    mu = jnp.mean(y, axis=-1, keepdims=True)
    yc = y - mu
    var = jnp.mean(yc * yc, axis=-1, keepdims=True)
    z = _silu(yc * lax.rsqrt(var + EPS) * gln_ref[...] + bln_ref[...])
    return _dot(z.astype(BF16), wpw_ref[...]) + bpw_ref[...]


def _inproj_body(x_ref, mod_ref, g_ref, cos_ref, sin_ref, whi_ref,
                 wdw_ref, bdw_ref, gln_ref, bln_ref, wpw_ref, bpw_ref, wout_ref,
                 yo_ref, qt_ref, k_ref, vt_ref, sga_ref, bias_ref,
                 km_ref, win_ref, conv_ref):
    i = pl.program_id(1)

    @pl.when(i == 0)
    def _():
        km_ref[...] = jnp.zeros_like(km_ref)
        win_ref[0, 0:HALO, :] = jnp.zeros((HALO, D_CONV), F32)

    x = x_ref[0]
    r = lax.rsqrt(jnp.mean(x * x, axis=-1, keepdims=True) + EPS)
    shift = mod_ref[0, 0:1, :]
    scale = mod_ref[0, 1:2, :]
    h = (x * r) * (g_ref[...] * (1.0 + scale)) + shift
    h_hi = h.astype(BF16)

    def w_hi(c0):
        return whi_ref[:, c0:c0 + 512]

    u = _dot(h_hi, w_hi(C_VAL)) * _sigmoid(_dot(h_hi, w_hi(C_GLU)))
    conv_steps = _conv_steps(u, win_ref, conv_ref, wdw_ref)
    columns = (C_GATE, C_V, C_AG, C_Q, C_K)
    products = []
    for n, c0 in enumerate(columns):
        products.append(_dot(h_hi, w_hi(c0)))
        while len(conv_steps) * len(columns) > (len(columns) - 1 - n) * (1 + BLOCK // CONV_CHUNK):
            conv_steps.pop(0)(products[-1])
    c_gate, v, a_gate, q, k = products
    conv_out = _conv_finish(win_ref, conv_ref, bdw_ref, gln_ref, bln_ref, wpw_ref, bpw_ref)

    vt = v.T.astype(BF16)
    ones = jnp.ones((V_ROWS - HEAD_DIM, BLOCK), BF16)
    vt_ref[...] = jnp.concatenate(
        [piece for hd in range(N_HEADS) for piece in (vt[hd * HEAD_DIM:(hd + 1) * HEAD_DIM], ones)], axis=0)
    sga_ref[...] = _silu(a_gate).astype(BF16)

    cos = cos_ref[...]
    sin = sin_ref[...]
    qt = _rope_t(q.T, cos, sin) * Q_SCALE
    kr = _rope_t(k.T, cos, sin).T
    qt_hi = qt.astype(BF16)
    qt_ref[...] = qt_hi
    k_ref[...] = kr.astype(BF16)

    km = km_ref[...]
    lane_k = lax.broadcasted_iota(jnp.int32, km.shape, 1)
    kmt = jnp.concatenate([jnp.where((lane_k // HEAD_DIM) == hd, km, 0.0) for hd in range(N_HEADS)], axis=0)
    gt = _dot(kmt.astype(BF16), qt_hi)

    y_conv = conv_out * _silu(c_gate)
    yo_ref[...] = _dot(y_conv.astype(BF16), wout_ref[...]).astype(BF16)

    nb = km.shape[0]
    n_iota = lax.broadcasted_iota(jnp.int32, (nb, BLOCK), 0)
    past = n_iota < i
    n_sel = jnp.minimum(i, TOP_K)
    biases = []
    for hd in range(N_HEADS):
        g = jnp.where(past, gt[hd * nb:(hd + 1) * nb, :], NEG)
        rank = jnp.zeros((nb, BLOCK), jnp.int32)
        for n2 in range(nb):
            row = g[n2:n2 + 1, :]
            beats = (row > g) | ((row == g) & (n_iota > n2))
            rank = rank + jnp.where(beats, 1, 0)
        biases.append(jnp.where(rank < n_sel, 0.0, NEG))
    bias_ref[...] = jnp.concatenate(biases, axis=0).astype(BF16)

    km_ref[pl.ds(i, 1), :] = jnp.mean(kr, axis=0, keepdims=True)


def _inproj(x, mod3, g_norm, cos_t, sin_t, w_hi, w_dw, b_dw, g_ln, b_ln, w_pw_bf, b_pw, w_out_conv_bf):
    bsz, seq, _ = x.shape
    nblk = seq // BLOCK
    n = bsz * seq
    row = lambda b, i: (b * nblk + i, 0)
    col = lambda b, i: (0, b * nblk + i)
    const = lambda b, i: (0, 0)
    rows_out = pl.BlockSpec((BLOCK, 512), row)
    rows_sd = jax.ShapeDtypeStruct((n, 512), BF16)
    vec = pl.BlockSpec((1, D_CONV), const)
    return pl.pallas_call(
        _inproj_body,
        grid=(bsz, nblk),
        in_specs=[pl.BlockSpec((1, BLOCK, D_MODEL), lambda b, i: (b, i, 0)),
                  pl.BlockSpec((1, 3, D_MODEL), lambda b, i: (b, 0, 0)),
                  pl.BlockSpec((1, D_MODEL), const),
                  pl.BlockSpec((ROPE_HALF, BLOCK), col),
                  pl.BlockSpec((ROPE_HALF, BLOCK), col),
                  pl.BlockSpec((D_MODEL, D_IN), const),
                  pl.BlockSpec((CONV_WIDTH, SUBLANES, D_CONV), lambda b, i: (0, 0, 0)),
                  vec, vec, vec,
                  pl.BlockSpec((D_CONV, D_CONV), const),
                  vec,
                  pl.BlockSpec((D_CONV, D_MODEL), const)],
        out_specs=[pl.BlockSpec((BLOCK, D_MODEL), row),
                   pl.BlockSpec((D_ATTN, BLOCK), col),
                   rows_out,
                   pl.BlockSpec((N_HEADS * V_ROWS, BLOCK), col),
                   rows_out,
                   pl.BlockSpec((LANES, BLOCK), col)],
        out_shape=[jax.ShapeDtypeStruct((n, D_MODEL), BF16),
                   jax.ShapeDtypeStruct((D_ATTN, n), BF16),
                   rows_sd,
                   jax.ShapeDtypeStruct((N_HEADS * V_ROWS, n), BF16),
                   rows_sd,
                   jax.ShapeDtypeStruct((LANES, n), BF16)],
        scratch_shapes=[pltpu.VMEM((nblk, D_ATTN), F32),
                        pltpu.VMEM((SUBLANES, CONV_ROWS, D_CONV), F32),
                        pltpu.VMEM((BLOCK, D_CONV), F32)],
        compiler_params=pltpu.CompilerParams(dimension_semantics=("arbitrary", "arbitrary"),
                                             vmem_limit_bytes=VMEM_LIMIT),
        name="inproj",
    )(x, mod3, g_norm, cos_t, sin_t, w_hi,
      jnp.broadcast_to(w_dw[:, None, :], (CONV_WIDTH, SUBLANES, D_CONV)),
      b_dw.reshape(1, -1), g_ln.reshape(1, -1), b_ln.reshape(1, -1), w_pw_bf, b_pw.reshape(1, -1),
      w_out_conv_bf)


def _attn_body(qt_ref, bias_ref, k_ref, vt_ref, sga_ref, onehot_ref, x_ref, mod_ref, yo_ref, wout_ref, gf_ref,
               o_ref, qa_ref, s_ref, smax_ref, m_ref, l_ref, acc_ref):
    i = pl.program_id(1)
    zeros_q = jnp.zeros((HEAD_DIM, BLOCK), BF16)
    for h in range(N_HEADS):
        qh = qt_ref[h * HEAD_DIM:(h + 1) * HEAD_DIM, :]
        bh = bias_ref[h * SEL_LANES:(h + 1) * SEL_LANES, :]
        pieces = [qh, zeros_q] if h % 2 == 0 else [zeros_q, qh]
        if h > 0:
            pieces.append(jnp.zeros((h * SEL_LANES, BLOCK), BF16))
        pieces.append(bh)
        if h < N_HEADS - 1:
            pieces.append(jnp.zeros(((N_HEADS - 1 - h) * SEL_LANES, BLOCK), BF16))
        qa_ref[h] = jnp.concatenate(pieces, axis=0)

    def score_fn(j, own):
        start = pl.multiple_of(j * BLOCK, BLOCK)
        sel = jnp.zeros((BLOCK, LANES), BF16) if own else onehot_ref[j]
        operands = {}

        def f(h):
            pair = h // 2
            if pair not in operands:
                kj = k_ref[pl.ds(start, BLOCK), pair * LANES:(pair + 1) * LANES]
                operands[pair] = jnp.concatenate([kj, sel], axis=1)
            return _dot(operands[pair], qa_ref[h])
        return f

    def head_step(j, h, st, m_cur, first):
        start = pl.multiple_of(j * BLOCK, BLOCK)
        if first:
            m_new = m_cur
        else:
            m_prev = m_ref[h:h + 1, :]
            m_new = jnp.maximum(m_prev, m_cur)
            alpha = jnp.exp2(m_prev - m_new)
        pt = jnp.exp2(st - m_new)
        vth = vt_ref[h * V_ROWS:(h + 1) * V_ROWS, pl.ds(start, BLOCK)]
        pv_l = _dot(vth, pt.astype(BF16))
        pv = pv_l[0:HEAD_DIM]
        l_cur = pv_l[HEAD_DIM:HEAD_DIM + 1]
        rows = slice(h * HEAD_DIM, (h + 1) * HEAD_DIM)
        acc_ref[rows, :] = pv if first else alpha * acc_ref[rows, :] + pv
        l_ref[h:h + 1, :] = l_cur if first else alpha * l_ref[h:h + 1, :] + l_cur
        m_ref[h:h + 1, :] = m_new

    key_idx = lax.broadcasted_iota(jnp.int32, (BLOCK, BLOCK), 0)
    qry_idx = lax.broadcasted_iota(jnp.int32, (BLOCK, BLOCK), 1)
    causal = key_idx <= qry_idx
    own_fn = score_fn(i, True)
    own = [own_fn(h) for h in range(N_HEADS)]

    def overlapped(slot_next, j_next, j_cur, cur, first):
        nxt = score_fn(j_next, False)
        for h in range(N_HEADS):
            st_next = nxt(h)
            s_ref[slot_next, h] = st_next
            smax_ref[slot_next, h:h + 1, :] = jnp.max(st_next, axis=0, keepdims=True)
            head_step(j_cur, h, *cur(h), first)

    odd = i % 2
    def own_scores(h):
        st = jnp.where(causal, own[h], NEG)
        return st, jnp.max(st, axis=0, keepdims=True)

    def stored(slot):
        return lambda h: (s_ref[slot, h], smax_ref[slot, h:h + 1, :])

    overlapped(odd, 0, i, own_scores, True)

    @pl.when(odd == 1)
    def _():
        overlapped(0, 1, 0, stored(1), False)

    def run_blocks(j0, count):
        for d in range(0, count, 2):
            overlapped(1, j0 + d + 1, j0 + d, stored(0), False)
            overlapped(0, jnp.minimum(j0 + d + 2, i), j0 + d + 1, stored(1), False)

    pairs = i // 2

    @pl.when(pairs % 2 == 1)
    def _():
        run_blocks(odd, 2)

    def quad_body(t, carry):
        run_blocks(odd + 2 * (pairs % 2) + ATTN_UNROLL * t, ATTN_UNROLL)
        return carry

    lax.fori_loop(0, pairs // 2, quad_body, 0)

    for half in range(2):
        qs = slice(half * LANES, (half + 1) * LANES)
        outs = [acc_ref[h * HEAD_DIM:(h + 1) * HEAD_DIM, qs] * (1.0 / l_ref[h:h + 1, qs]) for h in range(N_HEADS)]
        out = jnp.concatenate(outs, axis=0).T
        y_attn = (out * sga_ref[qs, :].astype(F32)).astype(BF16)
        y = yo_ref[qs, :].astype(F32) + _dot(y_attn, wout_ref[...])
        z = x_ref[0, qs, :] + mod_ref[0, 2:3, :] * y
        r = lax.rsqrt(jnp.mean(z * z, axis=-1, keepdims=True) + EPS)
        o_ref[0, qs, :] = z * r * gf_ref[...]


def _attention_outproj(qt, bias_t, k, vt, sga, x, mod3, y_out_conv, w_out_attn_bf, g_final):
    bsz, seq, _ = x.shape
    nblk = seq // BLOCK
    assert nblk == SEL_LANES, "one selection lane per key block"
    row = lambda b, i: (b * nblk + i, 0)
    col = lambda b, i: (0, b * nblk + i)
    const = lambda b, i: (0, 0)
    xspec = pl.BlockSpec((1, BLOCK, D_MODEL), lambda b, i: (b, i, 0))
    lane = lax.broadcasted_iota(jnp.int32, (nblk, BLOCK, LANES), 2)
    blk = lax.broadcasted_iota(jnp.int32, (nblk, BLOCK, LANES), 0)
    onehot = ((lane % SEL_LANES) == blk).astype(BF16)
    return pl.pallas_call(
        _attn_body,
        grid=(bsz, nblk),
        in_specs=[pl.BlockSpec((D_ATTN, BLOCK), col),
                  pl.BlockSpec((LANES, BLOCK), col),
                  pl.BlockSpec((seq, D_ATTN), lambda b, i: (b, 0)),
                  pl.BlockSpec((N_HEADS * V_ROWS, seq), lambda b, i: (0, b)),
                  pl.BlockSpec((BLOCK, D_ATTN), row),
                  pl.BlockSpec((nblk, BLOCK, LANES), lambda b, i: (0, 0, 0)),
                  xspec,
                  pl.BlockSpec((1, 3, D_MODEL), lambda b, i: (b, 0, 0)),
                  pl.BlockSpec((BLOCK, D_MODEL), row),
                  pl.BlockSpec((D_ATTN, D_MODEL), const),
                  pl.BlockSpec((1, D_MODEL), const)],
        out_specs=xspec,
        out_shape=jax.ShapeDtypeStruct(x.shape, F32),
        scratch_shapes=[pltpu.VMEM((N_HEADS, 2 * LANES, BLOCK), BF16),
                        pltpu.VMEM((2, N_HEADS, BLOCK, BLOCK), F32),
                        pltpu.VMEM((2, N_HEADS, BLOCK), F32),
                        pltpu.VMEM((N_HEADS, BLOCK), F32),
                        pltpu.VMEM((N_HEADS, BLOCK), F32),
                        pltpu.VMEM((D_ATTN, BLOCK), F32)],
        compiler_params=pltpu.CompilerParams(dimension_semantics=("arbitrary", "arbitrary"),
                                             vmem_limit_bytes=VMEM_LIMIT),
        name="attention_outproj",
    )(qt, bias_t, k, vt, sga, onehot, x, mod3, y_out_conv, w_out_attn_bf, g_final.reshape(1, -1))


def kernel(x, c, positions, w_ada, b_ada, g_norm, w_in, w_dw, b_dw, g_ln_conv, b_ln_conv, w_pw, b_pw,
           w_out, g_final):
    bsz, seq, _ = x.shape
    assert w_ada.shape[0] == 1, "single-layer block: the final rmsnorm is fused into the last kernel"
    cos_t, sin_t = _rope_tables(positions)
    mod3 = _adaln(c, w_ada[0], b_ada[0]).reshape(bsz, 3, D_MODEL)
    w_hi = w_in[0].astype(BF16)
    w_out_bf = w_out[0].astype(BF16)
    y_out_conv, qt, k, vt, sga, bias_t = _inproj(x, mod3, g_norm[0].reshape(1, -1), cos_t, sin_t, w_hi,
                                                 w_dw[0], b_dw[0], g_ln_conv[0], b_ln_conv[0],
                                                 w_pw[0].astype(BF16), b_pw[0], w_out_bf[:D_CONV])
    return _attention_outproj(qt, bias_t, k, vt, sga, x, mod3, y_out_conv, w_out_bf[D_CONV:], g_final)
```

```python
import jax
import jax.numpy as jnp
from jax import lax
from jax.experimental import pallas as pl
from jax.experimental.pallas import tpu as pltpu

F32 = jnp.float32
BF16 = jnp.bfloat16

D_MODEL = 1024
D_CONV = 512
D_ATTN = 512
HEAD_DIM = 64
N_HEADS = 8
CONV_WIDTH = 31
BLOCK = 256
TOP_K = 3
ROPE_DIM = 16
ROPE_HALF = 8
ROPE_THETA = 500000.0
EPS = 1e-6
NEG = -1e30
Q_SCALE = HEAD_DIM ** -0.5 * 1.4426950408889634
D_IN = 3584
C_VAL, C_GLU, C_GATE, C_Q, C_K, C_V, C_AG = 0, 512, 1024, 1536, 2048, 2560, 3072

LANES = 128
SUBLANES = 8
SEL_LANES = LANES // N_HEADS
V_ROWS = HEAD_DIM + 16
HALO = 32
CONV_ROWS = BLOCK + HALO
CONV_CHUNK = 16
ATTN_UNROLL = 4
VMEM_LIMIT = 48 * 1024 * 1024


def _split_bf16(a):
    hi = a.astype(BF16)
    lo = (a - hi.astype(F32)).astype(BF16)
    return hi, lo


def _dot(a, b):
    return jnp.dot(a, b, preferred_element_type=F32)


def _dot3(a_hi, a_lo, b_hi, b_lo):
    return _dot(a_hi, b_hi) + (_dot(a_hi, b_lo) + _dot(a_lo, b_hi))


def _sigmoid(a):
    return 1.0 / (1.0 + jnp.exp(-a))


def _silu(a):
    return a * _sigmoid(a)


def _adaln_body(c_ref, w_ref, b_ref, o_ref):
    a_hi, a_lo = _split_bf16(_silu(c_ref[...]))
    w_hi, w_lo = _split_bf16(w_ref[...])
    o_ref[...] = _dot3(a_hi, a_lo, w_hi, w_lo) + b_ref[...]


def _adaln(c, w_ada, b_ada):
    bsz = c.shape[0]
    n_out = w_ada.shape[1]
    tn = 1024
    return pl.pallas_call(
        _adaln_body,
        grid=(n_out // tn,),
        in_specs=[pl.BlockSpec((bsz, D_MODEL), lambda j: (0, 0)),
                  pl.BlockSpec((D_MODEL, tn), lambda j: (0, j)),
                  pl.BlockSpec((1, tn), lambda j: (0, j))],
        out_specs=pl.BlockSpec((bsz, tn), lambda j: (0, j)),
        out_shape=jax.ShapeDtypeStruct((bsz, n_out), F32),
        compiler_params=pltpu.CompilerParams(dimension_semantics=("arbitrary",),
                                             vmem_limit_bytes=VMEM_LIMIT),
        name="adaln",
    )(c, w_ada, b_ada.reshape(1, n_out))


def _rope_table_body(inv_ref, pos_ref, cos_ref, sin_ref):
    pos = pos_ref[...].astype(F32)
    for i in range(ROPE_HALF):
        ang = pos * inv_ref[i]
        cos_ref[i] = jnp.cos(ang)
        sin_ref[i] = jnp.sin(ang)


def _rope_tables(positions):
    bsz, seq = positions.shape
    inv = ROPE_THETA ** (-(jnp.arange(ROPE_HALF, dtype=F32) * 2.0) / ROPE_DIM)
    cos, sin = pl.pallas_call(
        _rope_table_body,
        in_specs=[pl.BlockSpec(memory_space=pltpu.SMEM),
                  pl.BlockSpec((bsz, seq), lambda: (0, 0))],
        out_specs=[pl.BlockSpec((ROPE_HALF, bsz, seq), lambda: (0, 0, 0))] * 2,
        out_shape=[jax.ShapeDtypeStruct((ROPE_HALF, bsz, seq), F32)] * 2,
        name="rope_table",
    )(inv, positions)
    return cos.reshape(ROPE_HALF, bsz * seq), sin.reshape(ROPE_HALF, bsz * seq)


def _rope_t(t, cos, sin):
    pieces = []
    for hd in range(N_HEADS):
        r0 = hd * HEAD_DIM
        t1 = t[r0:r0 + ROPE_HALF]
        t2 = t[r0 + ROPE_HALF:r0 + ROPE_DIM]
        pieces += [t1 * cos - t2 * sin, t2 * cos + t1 * sin, t[r0 + ROPE_DIM:r0 + HEAD_DIM]]
    return jnp.concatenate(pieces, axis=0)


def _conv_steps(u, win_ref, conv_ref, wdw_ref):
    def setup():
        win_ref[0, HALO:, :] = u
        for s in range(1, SUBLANES):
            win_ref[s, 0:CONV_ROWS - SUBLANES, :] = win_ref[0, s:s + CONV_ROWS - SUBLANES, :]

    def chunk(c, after):
        r0 = c * CONV_CHUNK
        base = HALO - (CONV_WIDTH - 1)
        zero = jnp.minimum(jnp.maximum(after[0:CONV_CHUNK, :], 0.0), 0.0)
        acc = jnp.concatenate([zero] * (D_CONV // zero.shape[1]), axis=1)
        for w in range(CONV_WIDTH):
            a, s = divmod(base + w, SUBLANES)
            rows = slice(r0 + a * SUBLANES, r0 + a * SUBLANES + CONV_CHUNK)
            tap = jnp.concatenate([wdw_ref[w]] * (CONV_CHUNK // SUBLANES), axis=0)
            acc = acc + win_ref[s, rows, :] * tap
        conv_ref[r0:r0 + CONV_CHUNK, :] = acc

    return [lambda after: setup()] + [lambda after, c=c: chunk(c, after) for c in range(BLOCK // CONV_CHUNK)]


def _conv_finish(gate_act, win_ref, conv_ref, bdw_ref, gln_ref, bln_ref, wpw_ref, bpw_ref):
    win_ref[0, 0:HALO, :] = win_ref[0, BLOCK:BLOCK + HALO, :]
    y = conv_ref[...] + bdw_ref[...]
    mu = jnp.mean(y, axis=-1, keepdims=True)
    yc = y - mu
    var = jnp.mean(yc * yc, axis=-1, keepdims=True)
    z = _silu(yc * lax.rsqrt(var + EPS) * gln_ref[...] + bln_ref[...])
    return (_dot(z.astype(BF16), wpw_ref[...]) + bpw_ref[...]) * gate_act


def _inproj_body(x_ref, mod_ref, g_ref, cos_ref, sin_ref, whi_ref,
                 wdw_ref, bdw_ref, gln_ref, bln_ref, wpw_ref, bpw_ref,
                 yc_ref, qt_ref, k_ref, vt_ref, sga_ref, bias_ref,
                 km_ref, win_ref, conv_ref):
    i = pl.program_id(1)

    @pl.when(i == 0)
    def _():
        km_ref[...] = jnp.zeros_like(km_ref)
        win_ref[0, 0:HALO, :] = jnp.zeros((HALO, D_CONV), F32)

    x = x_ref[0]
    r = lax.rsqrt(jnp.mean(x * x, axis=-1, keepdims=True) + EPS)
    shift = mod_ref[0, 0:1, :]
    scale = mod_ref[0, 1:2, :]
    h = (x * r) * (g_ref[...] * (1.0 + scale)) + shift
    h_hi = h.astype(BF16)

    def w_hi(c0):
        return whi_ref[:, c0:c0 + 512]

    u = _dot(h_hi, w_hi(C_VAL)) * _sigmoid(_dot(h_hi, w_hi(C_GLU)))
    conv_steps = _conv_steps(u, win_ref, conv_ref, wdw_ref)
    columns = (C_GATE, C_V, C_AG, C_Q, C_K)
    products = []
    for n, c0 in enumerate(columns):
        products.append(_dot(h_hi, w_hi(c0)))
        while len(conv_steps) * len(columns) > (len(columns) - 1 - n) * (1 + BLOCK // CONV_CHUNK):
            conv_steps.pop(0)(products[-1])
    c_gate, v, a_gate, q, k = products
    y_conv = _conv_finish(_silu(c_gate), win_ref, conv_ref, bdw_ref, gln_ref, bln_ref, wpw_ref, bpw_ref)
    yc_ref[...] = y_conv.astype(BF16)

    vt = v.T.astype(BF16)
    ones = jnp.ones((V_ROWS - HEAD_DIM, BLOCK), BF16)
    vt_ref[...] = jnp.concatenate(
        [piece for hd in range(N_HEADS) for piece in (vt[hd * HEAD_DIM:(hd + 1) * HEAD_DIM], ones)], axis=0)
    sga_ref[...] = _silu(a_gate).astype(BF16)

    cos = cos_ref[...]
    sin = sin_ref[...]
    qt = _rope_t(q.T, cos, sin) * Q_SCALE
    kr = _rope_t(k.T, cos, sin).T
    qt_hi = qt.astype(BF16)
    qt_ref[...] = qt_hi
    k_ref[...] = kr.astype(BF16)

    km = km_ref[...]
    lane_k = lax.broadcasted_iota(jnp.int32, km.shape, 1)
    kmt = jnp.concatenate([jnp.where((lane_k // HEAD_DIM) == hd, km, 0.0) for hd in range(N_HEADS)], axis=0)
    gt = _dot(kmt.astype(BF16), qt_hi)

    nb = km.shape[0]
    n_iota = lax.broadcasted_iota(jnp.int32, (nb, BLOCK), 0)
    past = n_iota < i
    n_sel = jnp.minimum(i, TOP_K)
    biases = []
    for hd in range(N_HEADS):
        g = jnp.where(past, gt[hd * nb:(hd + 1) * nb, :], NEG)
        rank = jnp.zeros((nb, BLOCK), jnp.int32)
        for n2 in range(nb):
            row = g[n2:n2 + 1, :]
            beats = (row > g) | ((row == g) & (n_iota > n2))
            rank = rank + jnp.where(beats, 1, 0)
        biases.append(jnp.where(rank < n_sel, 0.0, NEG))
    bias_ref[...] = jnp.concatenate(biases, axis=0).astype(BF16)

    km_ref[pl.ds(i, 1), :] = jnp.mean(kr, axis=0, keepdims=True)


def _inproj(x, mod3, g_norm, cos_t, sin_t, w_hi, w_dw, b_dw, g_ln, b_ln, w_pw_bf, b_pw):
    bsz, seq, _ = x.shape
    nblk = seq // BLOCK
    n = bsz * seq
    row = lambda b, i: (b * nblk + i, 0)
    col = lambda b, i: (0, b * nblk + i)
    const = lambda b, i: (0, 0)
    rows_out = pl.BlockSpec((BLOCK, 512), row)
    rows_sd = jax.ShapeDtypeStruct((n, 512), BF16)
    vec = pl.BlockSpec((1, D_CONV), const)
    return pl.pallas_call(
        _inproj_body,
        grid=(bsz, nblk),
        in_specs=[pl.BlockSpec((1, BLOCK, D_MODEL), lambda b, i: (b, i, 0)),
                  pl.BlockSpec((1, 3, D_MODEL), lambda b, i: (b, 0, 0)),
                  pl.BlockSpec((1, D_MODEL), const),
                  pl.BlockSpec((ROPE_HALF, BLOCK), col),
                  pl.BlockSpec((ROPE_HALF, BLOCK), col),
                  pl.BlockSpec((D_MODEL, D_IN), const),
                  pl.BlockSpec((CONV_WIDTH, SUBLANES, D_CONV), lambda b, i: (0, 0, 0)),
                  vec, vec, vec,
                  pl.BlockSpec((D_CONV, D_CONV), const),
                  vec],
        out_specs=[rows_out,
                   pl.BlockSpec((D_ATTN, BLOCK), col),
                   rows_out,
                   pl.BlockSpec((N_HEADS * V_ROWS, BLOCK), col),
                   rows_out,
                   pl.BlockSpec((LANES, BLOCK), col)],
        out_shape=[rows_sd,
                   jax.ShapeDtypeStruct((D_ATTN, n), BF16),
                   rows_sd,
                   jax.ShapeDtypeStruct((N_HEADS * V_ROWS, n), BF16),
                   rows_sd,
                   jax.ShapeDtypeStruct((LANES, n), BF16)],
        scratch_shapes=[pltpu.VMEM((nblk, D_ATTN), F32),
                        pltpu.VMEM((SUBLANES, CONV_ROWS, D_CONV), F32),
                        pltpu.VMEM((BLOCK, D_CONV), F32)],
        compiler_params=pltpu.CompilerParams(dimension_semantics=("arbitrary", "arbitrary"),
                                             vmem_limit_bytes=VMEM_LIMIT),
        name="inproj",
    )(x, mod3, g_norm, cos_t, sin_t, w_hi,
      jnp.broadcast_to(w_dw[:, None, :], (CONV_WIDTH, SUBLANES, D_CONV)),
      b_dw.reshape(1, -1), g_ln.reshape(1, -1), b_ln.reshape(1, -1), w_pw_bf, b_pw.reshape(1, -1))


def _attn_body(qt_ref, bias_ref, k_ref, vt_ref, sga_ref, onehot_ref, x_ref, mod_ref, yc_ref, wout_ref, gf_ref,
               o_ref, qa_ref, s_ref, smax_ref, m_ref, l_ref, acc_ref, yo_ref):
    i = pl.program_id(1)
    yo_ref[...] = _dot(yc_ref[...], wout_ref[0:D_CONV, :])
    zeros_q = jnp.zeros((HEAD_DIM, BLOCK), BF16)
    for h in range(N_HEADS):
        qh = qt_ref[h * HEAD_DIM:(h + 1) * HEAD_DIM, :]
        bh = bias_ref[h * SEL_LANES:(h + 1) * SEL_LANES, :]
        pieces = [qh, zeros_q] if h % 2 == 0 else [zeros_q, qh]
        if h > 0:
            pieces.append(jnp.zeros((h * SEL_LANES, BLOCK), BF16))
        pieces.append(bh)
        if h < N_HEADS - 1:
            pieces.append(jnp.zeros(((N_HEADS - 1 - h) * SEL_LANES, BLOCK), BF16))
        qa_ref[h] = jnp.concatenate(pieces, axis=0)

    def score_fn(j, own):
        start = pl.multiple_of(j * BLOCK, BLOCK)
        sel = jnp.zeros((BLOCK, LANES), BF16) if own else onehot_ref[j]
        operands = {}

        def f(h):
            pair = h // 2
            if pair not in operands:
                kj = k_ref[pl.ds(start, BLOCK), pair * LANES:(pair + 1) * LANES]
                operands[pair] = jnp.concatenate([kj, sel], axis=1)
            return _dot(operands[pair], qa_ref[h])
        return f

    def head_step(j, h, st, m_cur, first):
        start = pl.multiple_of(j * BLOCK, BLOCK)
        if first:
            m_new = m_cur
        else:
            m_prev = m_ref[h:h + 1, :]
            m_new = jnp.maximum(m_prev, m_cur)
            alpha = jnp.exp2(m_prev - m_new)
        pt = jnp.exp2(st - m_new)
        vth = vt_ref[h * V_ROWS:(h + 1) * V_ROWS, pl.ds(start, BLOCK)]
        pv_l = _dot(vth, pt.astype(BF16))
        pv = pv_l[0:HEAD_DIM]
        l_cur = pv_l[HEAD_DIM:HEAD_DIM + 1]
        rows = slice(h * HEAD_DIM, (h + 1) * HEAD_DIM)
        acc_ref[rows, :] = pv if first else alpha * acc_ref[rows, :] + pv
        l_ref[h:h + 1, :] = l_cur if first else alpha * l_ref[h:h + 1, :] + l_cur
        m_ref[h:h + 1, :] = m_new

    key_idx = lax.broadcasted_iota(jnp.int32, (BLOCK, BLOCK), 0)
    qry_idx = lax.broadcasted_iota(jnp.int32, (BLOCK, BLOCK), 1)
    causal = key_idx <= qry_idx
    own_fn = score_fn(i, True)
    own = [own_fn(h) for h in range(N_HEADS)]

    def overlapped(slot_next, j_next, j_cur, cur, first):
        nxt = score_fn(j_next, False)
        for h in range(N_HEADS):
            st_next = nxt(h)
            s_ref[slot_next, h] = st_next
            smax_ref[slot_next, h:h + 1, :] = jnp.max(st_next, axis=0, keepdims=True)
            head_step(j_cur, h, *cur(h), first)

    def own_scores(h):
        st = jnp.where(causal, own[h], NEG)
        return st, jnp.max(st, axis=0, keepdims=True)

    def stored(slot):
        return lambda h: (s_ref[slot, h], smax_ref[slot, h:h + 1, :])

    odd = i % 2
    overlapped(odd, 0, i, own_scores, True)

    @pl.when(odd == 1)
    def _():
        overlapped(0, 1, 0, stored(1), False)

    def run_blocks(j0, count):
        for d in range(0, count, 2):
            overlapped(1, j0 + d + 1, j0 + d, stored(0), False)
            overlapped(0, jnp.minimum(j0 + d + 2, i), j0 + d + 1, stored(1), False)

    pairs = i // 2

    @pl.when(pairs % 2 == 1)
    def _():
        run_blocks(odd, 2)

    def quad_body(t, carry):
        run_blocks(odd + 2 * (pairs % 2) + ATTN_UNROLL * t, ATTN_UNROLL)
        return carry

    lax.fori_loop(0, pairs // 2, quad_body, 0)

    for half in range(2):
        qs = slice(half * LANES, (half + 1) * LANES)
        outs = [acc_ref[h * HEAD_DIM:(h + 1) * HEAD_DIM, qs] * (1.0 / l_ref[h:h + 1, qs]) for h in range(N_HEADS)]
        out = jnp.concatenate(outs, axis=0).T
        y_attn = (out * sga_ref[qs, :].astype(F32)).astype(BF16)
        y = yo_ref[qs, :] + _dot(y_attn, wout_ref[D_CONV:, :])
        z = x_ref[0, qs, :] + mod_ref[0, 2:3, :] * y
        r = lax.rsqrt(jnp.mean(z * z, axis=-1, keepdims=True) + EPS)
        o_ref[0, qs, :] = z * r * gf_ref[...]


def _attention_outproj(qt, bias_t, k, vt, sga, x, mod3, y_conv, w_out_bf, g_final):
    bsz, seq, _ = x.shape
    nblk = seq // BLOCK
    assert nblk == SEL_LANES, "one selection lane per key block"
    row = lambda b, i: (b * nblk + i, 0)
    col = lambda b, i: (0, b * nblk + i)
    const = lambda b, i: (0, 0)
    xspec = pl.BlockSpec((1, BLOCK, D_MODEL), lambda b, i: (b, i, 0))
    lane = lax.broadcasted_iota(jnp.int32, (nblk, BLOCK, LANES), 2)
    blk = lax.broadcasted_iota(jnp.int32, (nblk, BLOCK, LANES), 0)
    onehot = ((lane % SEL_LANES) == blk).astype(BF16)
    return pl.pallas_call(
        _attn_body,
        grid=(bsz, nblk),
        in_specs=[pl.BlockSpec((D_ATTN, BLOCK), col),
                  pl.BlockSpec((LANES, BLOCK), col),
                  pl.BlockSpec((seq, D_ATTN), lambda b, i: (b, 0)),
                  pl.BlockSpec((N_HEADS * V_ROWS, seq), lambda b, i: (0, b)),
                  pl.BlockSpec((BLOCK, D_ATTN), row),
                  pl.BlockSpec((nblk, BLOCK, LANES), lambda b, i: (0, 0, 0)),
                  xspec,
                  pl.BlockSpec((1, 3, D_MODEL), lambda b, i: (b, 0, 0)),
                  pl.BlockSpec((BLOCK, D_CONV), row),
                  pl.BlockSpec((D_MODEL, D_MODEL), const),
                  pl.BlockSpec((1, D_MODEL), const)],
        out_specs=xspec,
        out_shape=jax.ShapeDtypeStruct(x.shape, F32),
        scratch_shapes=[pltpu.VMEM((N_HEADS, 2 * LANES, BLOCK), BF16),
                        pltpu.VMEM((2, N_HEADS, BLOCK, BLOCK), F32),
                        pltpu.VMEM((2, N_HEADS, BLOCK), F32),
                        pltpu.VMEM((N_HEADS, BLOCK), F32),
                        pltpu.VMEM((N_HEADS, BLOCK), F32),
                        pltpu.VMEM((D_ATTN, BLOCK), F32),
                        pltpu.VMEM((BLOCK, D_MODEL), F32)],
        compiler_params=pltpu.CompilerParams(dimension_semantics=("arbitrary", "arbitrary"),
                                             vmem_limit_bytes=VMEM_LIMIT),
        name="attention_outproj",
    )(qt, bias_t, k, vt, sga, onehot, x, mod3, y_conv, w_out_bf, g_final.reshape(1, -1))


def kernel(x, c, positions, w_ada, b_ada, g_norm, w_in, w_dw, b_dw, g_ln_conv, b_ln_conv, w_pw, b_pw,
           w_out, g_final):
    bsz, seq, _ = x.shape
    assert w_ada.shape[0] == 1, "single-layer block: the final rmsnorm is fused into the last kernel"
    cos_t, sin_t = _rope_tables(positions)
    mod3 = _adaln(c, w_ada[0], b_ada[0]).reshape(bsz, 3, D_MODEL)
    w_hi = w_in[0].astype(BF16)
    y_conv, qt, k, vt, sga, bias_t = _inproj(x, mod3, g_norm[0].reshape(1, -1), cos_t, sin_t, w_hi,
                                             w_dw[0], b_dw[0], g_ln_conv[0], b_ln_conv[0],
                                             w_pw[0].astype(BF16), b_pw[0])
    return _attention_outproj(qt, bias_t, k, vt, sga, x, mod3, y_conv, w_out[0].astype(BF16), g_final)
```

```python
import jax
import jax.numpy as jnp
from jax import lax
from jax.experimental import pallas as pl
from jax.experimental.pallas import tpu as pltpu

F32 = jnp.float32
BF16 = jnp.bfloat16

D_MODEL = 1024
D_CONV = 512
D_ATTN = 512
HEAD_DIM = 64
N_HEADS = 8
CONV_WIDTH = 31
BLOCK = 256
TOP_K = 3
ROPE_DIM = 16
ROPE_HALF = 8
ROPE_THETA = 500000.0
EPS = 1e-6
NEG = -1e30
Q_SCALE = HEAD_DIM ** -0.5 * 1.4426950408889634
D_IN = 3584
C_VAL, C_GLU, C_GATE, C_Q, C_K, C_V, C_AG = 0, 512, 1024, 1536, 2048, 2560, 3072

LANES = 128
SUBLANES = 8
SEL_LANES = LANES // N_HEADS
V_ROWS = HEAD_DIM + 16
ROWS_W = D_CONV + 2 * D_ATTN
HALO = 32
CONV_ROWS = BLOCK + HALO
CONV_CHUNK = 16
ATTN_UNROLL = 4
VMEM_LIMIT = 48 * 1024 * 1024


def _split_bf16(a):
    hi = a.astype(BF16)
    lo = (a - hi.astype(F32)).astype(BF16)
    return hi, lo


def _dot(a, b):
    return jnp.dot(a, b, preferred_element_type=F32)


def _dot3(a_hi, a_lo, b_hi, b_lo):
    return _dot(a_hi, b_hi) + (_dot(a_hi, b_lo) + _dot(a_lo, b_hi))


def _sigmoid(a):
    return 1.0 / (1.0 + jnp.exp(-a))


def _silu(a):
    return a * _sigmoid(a)


def _adaln_body(c_ref, w_ref, b_ref, o_ref):
    a_hi, a_lo = _split_bf16(_silu(c_ref[...]))
    w_hi, w_lo = _split_bf16(w_ref[...])
    o_ref[...] = _dot3(a_hi, a_lo, w_hi, w_lo) + b_ref[...]


def _adaln(c, w_ada, b_ada):
    bsz = c.shape[0]
    n_out = w_ada.shape[1]
    tn = 1024
    return pl.pallas_call(
        _adaln_body,
        grid=(n_out // tn,),
        in_specs=[pl.BlockSpec((bsz, D_MODEL), lambda j: (0, 0)),
                  pl.BlockSpec((D_MODEL, tn), lambda j: (0, j)),
                  pl.BlockSpec((1, tn), lambda j: (0, j))],
        out_specs=pl.BlockSpec((bsz, tn), lambda j: (0, j)),
        out_shape=jax.ShapeDtypeStruct((bsz, n_out), F32),
        compiler_params=pltpu.CompilerParams(dimension_semantics=("arbitrary",),
                                             vmem_limit_bytes=VMEM_LIMIT),
        name="adaln",
    )(c, w_ada, b_ada.reshape(1, n_out))


def _rope_table_body(inv_ref, pos_ref, cs_ref):
    pos = pos_ref[...].astype(F32)
    for i in range(ROPE_HALF):
        ang = pos * inv_ref[i]
        cs_ref[i] = jnp.cos(ang)
        cs_ref[ROPE_HALF + i] = jnp.sin(ang)


def _rope_tables(positions):
    bsz, seq = positions.shape
    inv = ROPE_THETA ** (-(jnp.arange(ROPE_HALF, dtype=F32) * 2.0) / ROPE_DIM)
    cos_sin = pl.pallas_call(
        _rope_table_body,
        in_specs=[pl.BlockSpec(memory_space=pltpu.SMEM),
                  pl.BlockSpec((bsz, seq), lambda: (0, 0))],
        out_specs=pl.BlockSpec((2 * ROPE_HALF, bsz, seq), lambda: (0, 0, 0)),
        out_shape=jax.ShapeDtypeStruct((2 * ROPE_HALF, bsz, seq), F32),
        name="rope_table",
    )(inv, positions)
    return cos_sin.reshape(2 * ROPE_HALF, bsz * seq)


def _rope_t(t, cos, sin):
    pieces = []
    for hd in range(N_HEADS):
        r0 = hd * HEAD_DIM
        t1 = t[r0:r0 + ROPE_HALF]
        t2 = t[r0 + ROPE_HALF:r0 + ROPE_DIM]
        pieces += [t1 * cos - t2 * sin, t2 * cos + t1 * sin, t[r0 + ROPE_DIM:r0 + HEAD_DIM]]
    return jnp.concatenate(pieces, axis=0)


def _conv_steps(u, win_ref, conv_ref, wdw_ref):
    def setup():
        win_ref[0, HALO:, :] = u
        for s in range(1, SUBLANES):
            win_ref[s, 0:CONV_ROWS - SUBLANES, :] = win_ref[0, s:s + CONV_ROWS - SUBLANES, :]

    def chunk(c, after):
        r0 = c * CONV_CHUNK
        base = HALO - (CONV_WIDTH - 1)
        zero = jnp.minimum(jnp.maximum(after[0:CONV_CHUNK, :], 0.0), 0.0)
        acc = jnp.concatenate([zero] * (D_CONV // zero.shape[1]), axis=1)
        for w in range(CONV_WIDTH):
            a, s = divmod(base + w, SUBLANES)
            rows = slice(r0 + a * SUBLANES, r0 + a * SUBLANES + CONV_CHUNK)
            tap = jnp.concatenate([wdw_ref[w]] * (CONV_CHUNK // SUBLANES), axis=0)
            acc = acc + win_ref[s, rows, :] * tap
        conv_ref[r0:r0 + CONV_CHUNK, :] = acc

    return [lambda after: setup()] + [lambda after, c=c: chunk(c, after) for c in range(BLOCK // CONV_CHUNK)]


def _conv_finish(gate_act, win_ref, conv_ref, bdw_ref, gln_ref, bln_ref, wpw_ref, bpw_ref):
    win_ref[0, 0:HALO, :] = win_ref[0, BLOCK:BLOCK + HALO, :]
    y = conv_ref[...] + bdw_ref[...]
    mu = jnp.mean(y, axis=-1, keepdims=True)
    yc = y - mu
    var = jnp.mean(yc * yc, axis=-1, keepdims=True)
    z = _silu(yc * lax.rsqrt(var + EPS) * gln_ref[...] + bln_ref[...])
    return (_dot(z.astype(BF16), wpw_ref[...]) + bpw_ref[...]) * gate_act


def _inproj_body(x_ref, mod_ref, g_ref, cs_ref, whi_ref,
                 wdw_ref, bdw_ref, gln_ref, bln_ref, wpw_ref, bpw_ref,
                 rows_ref, qb_ref, vt_ref,
                 km_ref, win_ref, conv_ref):
    i = pl.program_id(1)

    @pl.when(i == 0)
    def _():
        km_ref[...] = jnp.zeros_like(km_ref)
        win_ref[0, 0:HALO, :] = jnp.zeros((HALO, D_CONV), F32)

    x = x_ref[0]
    r = lax.rsqrt(jnp.mean(x * x, axis=-1, keepdims=True) + EPS)
    shift = mod_ref[0, 0:1, :]
    scale = mod_ref[0, 1:2, :]
    h = (x * r) * (g_ref[...] * (1.0 + scale)) + shift
    h_hi = h.astype(BF16)

    def w_hi(c0):
        return whi_ref[:, c0:c0 + 512]

    u = _dot(h_hi, w_hi(C_VAL)) * _sigmoid(_dot(h_hi, w_hi(C_GLU)))
    conv_steps = _conv_steps(u, win_ref, conv_ref, wdw_ref)
    columns = (C_GATE, C_V, C_AG, C_Q, C_K)
    products = []
    for n, c0 in enumerate(columns):
        products.append(_dot(h_hi, w_hi(c0)))
        while len(conv_steps) * len(columns) > (len(columns) - 1 - n) * (1 + BLOCK // CONV_CHUNK):
            conv_steps.pop(0)(products[-1])
    c_gate, v, a_gate, q, k = products
    y_conv = _conv_finish(_silu(c_gate), win_ref, conv_ref, bdw_ref, gln_ref, bln_ref, wpw_ref, bpw_ref)
    rows_ref[:, 0:D_CONV] = y_conv.astype(BF16)

    vt = v.T.astype(BF16)
    ones = jnp.ones((V_ROWS - HEAD_DIM, BLOCK), BF16)
    vt_ref[...] = jnp.concatenate(
        [piece for hd in range(N_HEADS) for piece in (vt[hd * HEAD_DIM:(hd + 1) * HEAD_DIM], ones)], axis=0)
    rows_ref[:, D_CONV:D_CONV + D_ATTN] = _silu(a_gate).astype(BF16)

    cos = cs_ref[0:ROPE_HALF, :]
    sin = cs_ref[ROPE_HALF:, :]
    qt = _rope_t(q.T, cos, sin) * Q_SCALE
    kr = _rope_t(k.T, cos, sin).T
    qt_hi = qt.astype(BF16)
    qb_ref[0:D_ATTN, :] = qt_hi
    rows_ref[:, D_CONV + D_ATTN:] = kr.astype(BF16)

    km = km_ref[...]
    lane_k = lax.broadcasted_iota(jnp.int32, km.shape, 1)
    kmt = jnp.concatenate([jnp.where((lane_k // HEAD_DIM) == hd, km, 0.0) for hd in range(N_HEADS)], axis=0)
    gt = _dot(kmt.astype(BF16), qt_hi)

    nb = km.shape[0]
    n_iota = lax.broadcasted_iota(jnp.int32, (nb, BLOCK), 0)
    past = n_iota < i
    n_sel = jnp.minimum(i, TOP_K)
    biases = []
    for hd in range(N_HEADS):
        g = jnp.where(past, gt[hd * nb:(hd + 1) * nb, :], NEG)
        rank = jnp.zeros((nb, BLOCK), jnp.int32)
        for n2 in range(nb):
            row = g[n2:n2 + 1, :]
            beats = (row > g) | ((row == g) & (n_iota > n2))
            rank = rank + jnp.where(beats, 1, 0)
        biases.append(jnp.where(rank < n_sel, 0.0, NEG))
    qb_ref[D_ATTN:, :] = jnp.concatenate(biases, axis=0).astype(BF16)

    km_ref[pl.ds(i, 1), :] = jnp.mean(kr, axis=0, keepdims=True)


def _inproj(x, mod3, g_norm, cos_sin_t, w_hi, w_dw, b_dw, g_ln, b_ln, w_pw_bf, b_pw):
    bsz, seq, _ = x.shape
    nblk = seq // BLOCK
    n = bsz * seq
    row = lambda b, i: (b * nblk + i, 0)
    col = lambda b, i: (0, b * nblk + i)
    const = lambda b, i: (0, 0)
    vec = pl.BlockSpec((1, D_CONV), const)
    return pl.pallas_call(
        _inproj_body,
        grid=(bsz, nblk),
        in_specs=[pl.BlockSpec((1, BLOCK, D_MODEL), lambda b, i: (b, i, 0)),
                  pl.BlockSpec((1, 3, D_MODEL), lambda b, i: (b, 0, 0)),
                  pl.BlockSpec((1, D_MODEL), const),
                  pl.BlockSpec((2 * ROPE_HALF, BLOCK), col),
                  pl.BlockSpec((D_MODEL, D_IN), const),
                  pl.BlockSpec((CONV_WIDTH, SUBLANES, D_CONV), lambda b, i: (0, 0, 0)),
                  vec, vec, vec,
                  pl.BlockSpec((D_CONV, D_CONV), const),
                  vec],
        out_specs=[pl.BlockSpec((BLOCK, ROWS_W), row),
                   pl.BlockSpec((D_ATTN + LANES, BLOCK), col),
                   pl.BlockSpec((N_HEADS * V_ROWS, BLOCK), col)],
        out_shape=[jax.ShapeDtypeStruct((n, ROWS_W), BF16),
                   jax.ShapeDtypeStruct((D_ATTN + LANES, n), BF16),
                   jax.ShapeDtypeStruct((N_HEADS * V_ROWS, n), BF16)],
        scratch_shapes=[pltpu.VMEM((nblk, D_ATTN), F32),
                        pltpu.VMEM((SUBLANES, CONV_ROWS, D_CONV), F32),
                        pltpu.VMEM((BLOCK, D_CONV), F32)],
        compiler_params=pltpu.CompilerParams(dimension_semantics=("arbitrary", "arbitrary"),
                                             vmem_limit_bytes=VMEM_LIMIT),
        name="inproj",
    )(x, mod3, g_norm, cos_sin_t, w_hi,
      jnp.broadcast_to(w_dw[:, None, :], (CONV_WIDTH, SUBLANES, D_CONV)),
      b_dw.reshape(1, -1), g_ln.reshape(1, -1), b_ln.reshape(1, -1), w_pw_bf, b_pw.reshape(1, -1))


def _attn_body(qb_ref, k_ref, vt_ref, rows_ref, onehot_ref, x_ref, mod_ref, wout_ref, gf_ref,
               o_ref, qa_ref, s_ref, smax_ref, m_ref, l_ref, acc_ref, yo_ref):
    i = pl.program_id(1)
    yo_ref[...] = _dot(rows_ref[:, 0:D_CONV], wout_ref[0:D_CONV, :])
    zeros_q = jnp.zeros((HEAD_DIM, BLOCK), BF16)
    for h in range(N_HEADS):
        qh = qb_ref[h * HEAD_DIM:(h + 1) * HEAD_DIM, :]
        bh = qb_ref[D_ATTN + h * SEL_LANES:D_ATTN + (h + 1) * SEL_LANES, :]
        pieces = [qh, zeros_q] if h % 2 == 0 else [zeros_q, qh]
        if h > 0:
            pieces.append(jnp.zeros((h * SEL_LANES, BLOCK), BF16))
        pieces.append(bh)
        if h < N_HEADS - 1:
            pieces.append(jnp.zeros(((N_HEADS - 1 - h) * SEL_LANES, BLOCK), BF16))
        qa_ref[h] = jnp.concatenate(pieces, axis=0)

    def score_fn(j, own):
        start = pl.multiple_of(j * BLOCK, BLOCK)
        sel = jnp.zeros((BLOCK, LANES), BF16) if own else onehot_ref[j]
        operands = {}

        def f(h):
            pair = h // 2
            if pair not in operands:
                kj = k_ref[pl.ds(start, BLOCK), pair * LANES:(pair + 1) * LANES]
                operands[pair] = jnp.concatenate([kj, sel], axis=1)
            return _dot(operands[pair], qa_ref[h])
        return f

    def head_step(j, h, st, m_cur, first):
        start = pl.multiple_of(j * BLOCK, BLOCK)
        if first:
            m_new = m_cur
        else:
            m_prev = m_ref[h:h + 1, :]
            m_new = jnp.maximum(m_prev, m_cur)
            alpha = jnp.exp2(m_prev - m_new)
        pt = jnp.exp2(st - m_new)
        vth = vt_ref[h * V_ROWS:(h + 1) * V_ROWS, pl.ds(start, BLOCK)]
        pv_l = _dot(vth, pt.astype(BF16))
        pv = pv_l[0:HEAD_DIM]
        l_cur = pv_l[HEAD_DIM:HEAD_DIM + 1]
        rows = slice(h * HEAD_DIM, (h + 1) * HEAD_DIM)
        acc_ref[rows, :] = pv if first else alpha * acc_ref[rows, :] + pv
        l_ref[h:h + 1, :] = l_cur if first else alpha * l_ref[h:h + 1, :] + l_cur
        m_ref[h:h + 1, :] = m_new

    key_idx = lax.broadcasted_iota(jnp.int32, (BLOCK, BLOCK), 0)
    qry_idx = lax.broadcasted_iota(jnp.int32, (BLOCK, BLOCK), 1)
    causal = key_idx <= qry_idx
    own_fn = score_fn(i, True)
    own = [own_fn(h) for h in range(N_HEADS)]

    def overlapped(slot_next, j_next, j_cur, cur, first):
        nxt = score_fn(j_next, False)
        for h in range(N_HEADS):
            st_next = nxt(h)
            s_ref[slot_next, h] = st_next
            smax_ref[slot_next, h:h + 1, :] = jnp.max(st_next, axis=0, keepdims=True)
            head_step(j_cur, h, *cur(h), first)

    def own_scores(h):
        st = jnp.where(causal, own[h], NEG)
        return st, jnp.max(st, axis=0, keepdims=True)

    def stored(slot):
        return lambda h: (s_ref[slot, h], smax_ref[slot, h:h + 1, :])

    odd = i % 2
    overlapped(odd, 0, i, own_scores, True)

    @pl.when(odd == 1)
    def _():
        overlapped(0, 1, 0, stored(1), False)

    def run_blocks(j0, count):
        for d in range(0, count, 2):
            overlapped(1, j0 + d + 1, j0 + d, stored(0), False)
            overlapped(0, jnp.minimum(j0 + d + 2, i), j0 + d + 1, stored(1), False)

    pairs = i // 2

    @pl.when(pairs % 2 == 1)
    def _():
        run_blocks(odd, 2)

    def quad_body(t, carry):
        run_blocks(odd + 2 * (pairs % 2) + ATTN_UNROLL * t, ATTN_UNROLL)
        return carry

    lax.fori_loop(0, pairs // 2, quad_body, 0)

    for half in range(2):
        qs = slice(half * LANES, (half + 1) * LANES)
        outs = [acc_ref[h * HEAD_DIM:(h + 1) * HEAD_DIM, qs] * (1.0 / l_ref[h:h + 1, qs]) for h in range(N_HEADS)]
        out = jnp.concatenate(outs, axis=0).T
        y_attn = (out * rows_ref[qs, D_CONV:D_CONV + D_ATTN].astype(F32)).astype(BF16)
        y = yo_ref[qs, :] + _dot(y_attn, wout_ref[D_CONV:, :])
        z = x_ref[0, qs, :] + mod_ref[0, 2:3, :] * y
        r = lax.rsqrt(jnp.mean(z * z, axis=-1, keepdims=True) + EPS)
        o_ref[0, qs, :] = z * r * gf_ref[...]


def _attention_outproj(rows, qb, vt, x, mod3, w_out_bf, g_final):
    bsz, seq, _ = x.shape
    nblk = seq // BLOCK
    assert nblk == SEL_LANES, "one selection lane per key block"
    row = lambda b, i: (b * nblk + i, 0)
    col = lambda b, i: (0, b * nblk + i)
    const = lambda b, i: (0, 0)
    xspec = pl.BlockSpec((1, BLOCK, D_MODEL), lambda b, i: (b, i, 0))
    lane = lax.broadcasted_iota(jnp.int32, (nblk, BLOCK, LANES), 2)
    blk = lax.broadcasted_iota(jnp.int32, (nblk, BLOCK, LANES), 0)
    onehot = ((lane % SEL_LANES) == blk).astype(BF16)
    return pl.pallas_call(
        _attn_body,
        grid=(bsz, nblk),
        in_specs=[pl.BlockSpec((D_ATTN + LANES, BLOCK), col),
                  pl.BlockSpec((seq, D_ATTN), lambda b, i: (b, ROWS_W // D_ATTN - 1)),
                  pl.BlockSpec((N_HEADS * V_ROWS, seq), lambda b, i: (0, b)),
                  pl.BlockSpec((BLOCK, D_CONV + D_ATTN), row),
                  pl.BlockSpec((nblk, BLOCK, LANES), lambda b, i: (0, 0, 0)),
                  xspec,
                  pl.BlockSpec((1, 3, D_MODEL), lambda b, i: (b, 0, 0)),
                  pl.BlockSpec((D_MODEL, D_MODEL), const),
                  pl.BlockSpec((1, D_MODEL), const)],
        out_specs=xspec,
        out_shape=jax.ShapeDtypeStruct(x.shape, F32),
        scratch_shapes=[pltpu.VMEM((N_HEADS, 2 * LANES, BLOCK), BF16),
                        pltpu.VMEM((2, N_HEADS, BLOCK, BLOCK), F32),
                        pltpu.VMEM((2, N_HEADS, BLOCK), F32),
                        pltpu.VMEM((N_HEADS, BLOCK), F32),
                        pltpu.VMEM((N_HEADS, BLOCK), F32),
                        pltpu.VMEM((D_ATTN, BLOCK), F32),
                        pltpu.VMEM((BLOCK, D_MODEL), F32)],
        compiler_params=pltpu.CompilerParams(dimension_semantics=("arbitrary", "arbitrary"),
                                             vmem_limit_bytes=VMEM_LIMIT),
        name="attention_outproj",
    )(qb, rows, vt, rows, onehot, x, mod3, w_out_bf, g_final.reshape(1, -1))


def kernel(x, c, positions, w_ada, b_ada, g_norm, w_in, w_dw, b_dw, g_ln_conv, b_ln_conv, w_pw, b_pw,
           w_out, g_final):
    bsz, seq, _ = x.shape
    assert w_ada.shape[0] == 1, "single-layer block: the final rmsnorm is fused into the last kernel"
    cos_sin_t = _rope_tables(positions)
    mod3 = _adaln(c, w_ada[0], b_ada[0]).reshape(bsz, 3, D_MODEL)
    w_hi = w_in[0].astype(BF16)
    rows, qb, vt = _inproj(x, mod3, g_norm[0].reshape(1, -1), cos_sin_t, w_hi,
                           w_dw[0], b_dw[0], g_ln_conv[0], b_ln_conv[0], w_pw[0].astype(BF16), b_pw[0])
    return _attention_outproj(rows, qb, vt, x, mod3, w_out[0].astype(BF16), g_final)
```

```python
import jax
import jax.numpy as jnp
from jax import lax
from jax.experimental import pallas as pl
from jax.experimental.pallas import tpu as pltpu

F32 = jnp.float32
BF16 = jnp.bfloat16

D_MODEL = 1024
D_CONV = 512
D_ATTN = 512
HEAD_DIM = 64
N_HEADS = 8
CONV_WIDTH = 31
BLOCK = 256
TOP_K = 3
ROPE_DIM = 16
ROPE_HALF = 8
ROPE_THETA = 500000.0
EPS = 1e-6
NEG = -1e30
Q_SCALE = HEAD_DIM ** -0.5 * 1.4426950408889634
D_IN = 3584
C_VAL, C_GLU, C_GATE, C_Q, C_K, C_V, C_AG = 0, 512, 1024, 1536, 2048, 2560, 3072

LANES = 128
SUBLANES = 8
SEL_LANES = LANES // N_HEADS
V_ROWS = HEAD_DIM + 16
HALO = 32
CONV_ROWS = BLOCK + HALO
CONV_CHUNK = 32
ATTN_UNROLL = 4
VMEM_LIMIT = 48 * 1024 * 1024


def _split_bf16(a):
    hi = a.astype(BF16)
    lo = (a - hi.astype(F32)).astype(BF16)
    return hi, lo


def _dot(a, b):
    return jnp.dot(a, b, preferred_element_type=F32)


def _dot3(a_hi, a_lo, b_hi, b_lo):
    return _dot(a_hi, b_hi) + (_dot(a_hi, b_lo) + _dot(a_lo, b_hi))


def _sigmoid(a):
    return 1.0 / (1.0 + jnp.exp(-a))


def _silu(a):
    return a * _sigmoid(a)


def _adaln_body(c_ref, w_ref, b_ref, o_ref):
    a_hi, a_lo = _split_bf16(_silu(c_ref[...]))
    w_hi, w_lo = _split_bf16(w_ref[...])
    o_ref[...] = _dot3(a_hi, a_lo, w_hi, w_lo) + b_ref[...]


def _adaln(c, w_ada, b_ada):
    bsz = c.shape[0]
    n_out = w_ada.shape[1]
    tn = 1024
    return pl.pallas_call(
        _adaln_body,
        grid=(n_out // tn,),
        in_specs=[pl.BlockSpec((bsz, D_MODEL), lambda j: (0, 0)),
                  pl.BlockSpec((D_MODEL, tn), lambda j: (0, j)),
                  pl.BlockSpec((1, tn), lambda j: (0, j))],
        out_specs=pl.BlockSpec((bsz, tn), lambda j: (0, j)),
        out_shape=jax.ShapeDtypeStruct((bsz, n_out), F32),
        compiler_params=pltpu.CompilerParams(dimension_semantics=("arbitrary",),
                                             vmem_limit_bytes=VMEM_LIMIT),
        name="adaln",
    )(c, w_ada, b_ada.reshape(1, n_out))


def _rope_table_body(inv_ref, pos_ref, cos_ref, sin_ref):
    pos = pos_ref[...].astype(F32)
    for i in range(ROPE_HALF):
        ang = pos * inv_ref[i]
        cos_ref[i] = jnp.cos(ang)
        sin_ref[i] = jnp.sin(ang)


def _rope_tables(positions):
    bsz, seq = positions.shape
    inv = ROPE_THETA ** (-(jnp.arange(ROPE_HALF, dtype=F32) * 2.0) / ROPE_DIM)
    cos, sin = pl.pallas_call(
        _rope_table_body,
        in_specs=[pl.BlockSpec(memory_space=pltpu.SMEM),
                  pl.BlockSpec((bsz, seq), lambda: (0, 0))],
        out_specs=[pl.BlockSpec((ROPE_HALF, bsz, seq), lambda: (0, 0, 0))] * 2,
        out_shape=[jax.ShapeDtypeStruct((ROPE_HALF, bsz, seq), F32)] * 2,
        name="rope_table",
    )(inv, positions)
    return cos.reshape(ROPE_HALF, bsz * seq), sin.reshape(ROPE_HALF, bsz * seq)


def _rope_t(t, cos, sin):
    pieces = []
    for hd in range(N_HEADS):
        r0 = hd * HEAD_DIM
        t1 = t[r0:r0 + ROPE_HALF]
        t2 = t[r0 + ROPE_HALF:r0 + ROPE_DIM]
        pieces += [t1 * cos - t2 * sin, t2 * cos + t1 * sin, t[r0 + ROPE_DIM:r0 + HEAD_DIM]]
    return jnp.concatenate(pieces, axis=0)


def _conv_steps(u, win_ref, conv_ref, wdw_ref):
    def setup():
        win_ref[0, HALO:, :] = u
        for s in range(1, SUBLANES):
            win_ref[s, 0:CONV_ROWS - SUBLANES, :] = win_ref[0, s:s + CONV_ROWS - SUBLANES, :]

    def chunk(c, after):
        r0 = c * CONV_CHUNK
        base = HALO - (CONV_WIDTH - 1)
        zero = jnp.minimum(jnp.maximum(after[0:CONV_CHUNK, :], 0.0), 0.0)
        acc = jnp.concatenate([zero] * (D_CONV // zero.shape[1]), axis=1)
        for w in range(CONV_WIDTH):
            a, s = divmod(base + w, SUBLANES)
            rows = slice(r0 + a * SUBLANES, r0 + a * SUBLANES + CONV_CHUNK)
            tap = jnp.concatenate([wdw_ref[w]] * (CONV_CHUNK // SUBLANES), axis=0)
            acc = acc + win_ref[s, rows, :] * tap
        conv_ref[r0:r0 + CONV_CHUNK, :] = acc

    return [lambda after: setup()] + [lambda after, c=c: chunk(c, after) for c in range(BLOCK // CONV_CHUNK)]


def _conv_finish(gate_act, win_ref, conv_ref, bdw_ref, gln_ref, bln_ref, wpw_ref, bpw_ref):
    win_ref[0, 0:HALO, :] = win_ref[0, BLOCK:BLOCK + HALO, :]
    y = conv_ref[...] + bdw_ref[...]
    mu = jnp.mean(y, axis=-1, keepdims=True)
    yc = y - mu
    var = jnp.mean(yc * yc, axis=-1, keepdims=True)
    z = _silu(yc * lax.rsqrt(var + EPS) * gln_ref[...] + bln_ref[...])
    return (_dot(z.astype(BF16), wpw_ref[...]) + bpw_ref[...]) * gate_act


def _inproj_body(x_ref, mod_ref, g_ref, cos_ref, sin_ref, whi_ref,
                 wdw_ref, bdw_ref, gln_ref, bln_ref, wpw_ref, bpw_ref,
                 yc_ref, qt_ref, k_ref, vt_ref, sga_ref, bias_ref,
                 km_ref, win_ref, conv_ref):
    i = pl.program_id(1)

    @pl.when(i == 0)
    def _():
        km_ref[...] = jnp.zeros_like(km_ref)
        win_ref[0, 0:HALO, :] = jnp.zeros((HALO, D_CONV), F32)

    x = x_ref[0]
    r = lax.rsqrt(jnp.mean(x * x, axis=-1, keepdims=True) + EPS)
    shift = mod_ref[0, 0:1, :]
    scale = mod_ref[0, 1:2, :]
    h = (x * r) * (g_ref[...] * (1.0 + scale)) + shift
    h_hi = h.astype(BF16)

    def w_hi(c0):
        return whi_ref[:, c0:c0 + 512]

    u = _dot(h_hi, w_hi(C_VAL)) * _sigmoid(_dot(h_hi, w_hi(C_GLU)))
    plan = [(lambda: _dot(h_hi, w_hi(C_GATE)), 1), (lambda: _dot(h_hi, w_hi(C_V)), 2),
            (lambda: _dot(h_hi, w_hi(C_AG)), 2), (lambda: _dot(h_hi, w_hi(C_Q)), 2),
            (lambda: _dot(h_hi, w_hi(C_K)), 2)]
    conv_steps = _conv_steps(u, win_ref, conv_ref, wdw_ref)
    assert sum(n for _, n in plan) == len(conv_steps)
    products = []
    for matmul, n_steps in plan:
        products.append(matmul())
        for _ in range(n_steps):
            conv_steps.pop(0)(products[-1])
    c_gate, v, a_gate, q, k = products
    y_conv = _conv_finish(_silu(c_gate), win_ref, conv_ref, bdw_ref, gln_ref, bln_ref, wpw_ref, bpw_ref)
    yc_ref[...] = y_conv.astype(BF16)

    vt = v.T.astype(BF16)
    ones = jnp.ones((V_ROWS - HEAD_DIM, BLOCK), BF16)
    vt_ref[...] = jnp.concatenate(
        [piece for hd in range(N_HEADS) for piece in (vt[hd * HEAD_DIM:(hd + 1) * HEAD_DIM], ones)], axis=0)
    sga_ref[...] = _silu(a_gate).astype(BF16)

    cos = cos_ref[...]
    sin = sin_ref[...]
    qt = _rope_t(q.T, cos, sin) * Q_SCALE
    kr = _rope_t(k.T, cos, sin).T
    qt_hi = qt.astype(BF16)
    qt_ref[...] = qt_hi
    k_ref[...] = kr.astype(BF16)

    km = km_ref[...]
    lane_k = lax.broadcasted_iota(jnp.int32, km.shape, 1)
    kmt = jnp.concatenate([jnp.where((lane_k // HEAD_DIM) == hd, km, 0.0) for hd in range(N_HEADS)], axis=0)
    gt = _dot(kmt.astype(BF16), qt_hi)

    nb = km.shape[0]
    n_iota = lax.broadcasted_iota(jnp.int32, (nb, BLOCK), 0)
    past = n_iota < i
    n_sel = jnp.minimum(i, TOP_K)
    biases = []
    for hd in range(N_HEADS):
        g = jnp.where(past, gt[hd * nb:(hd + 1) * nb, :], NEG)
        rank = jnp.zeros((nb, BLOCK), jnp.int32)
        for n2 in range(nb):
            row = g[n2:n2 + 1, :]
            beats = (row > g) | ((row == g) & (n_iota > n2))
            rank = rank + jnp.where(beats, 1, 0)
        biases.append(jnp.where(rank < n_sel, 0.0, NEG))
    bias_ref[...] = jnp.concatenate(biases, axis=0).astype(BF16)

    km_ref[pl.ds(i, 1), :] = jnp.mean(kr, axis=0, keepdims=True)


def _inproj(x, mod3, g_norm, cos_t, sin_t, w_hi, w_dw, b_dw, g_ln, b_ln, w_pw_bf, b_pw):
    bsz, seq, _ = x.shape
    nblk = seq // BLOCK
    n = bsz * seq
    row = lambda b, i: (b * nblk + i, 0)
    col = lambda b, i: (0, b * nblk + i)
    const = lambda b, i: (0, 0)
    rows_out = pl.BlockSpec((BLOCK, 512), row)
    rows_sd = jax.ShapeDtypeStruct((n, 512), BF16)
    vec = pl.BlockSpec((1, D_CONV), const)
    return pl.pallas_call(
        _inproj_body,
        grid=(bsz, nblk),
        in_specs=[pl.BlockSpec((1, BLOCK, D_MODEL), lambda b, i: (b, i, 0)),
                  pl.BlockSpec((1, 3, D_MODEL), lambda b, i: (b, 0, 0)),
                  pl.BlockSpec((1, D_MODEL), const),
                  pl.BlockSpec((ROPE_HALF, BLOCK), col),
                  pl.BlockSpec((ROPE_HALF, BLOCK), col),
                  pl.BlockSpec((D_MODEL, D_IN), const),
                  pl.BlockSpec((CONV_WIDTH, SUBLANES, D_CONV), lambda b, i: (0, 0, 0)),
                  vec, vec, vec,
                  pl.BlockSpec((D_CONV, D_CONV), const),
                  vec],
        out_specs=[rows_out,
                   pl.BlockSpec((D_ATTN, BLOCK), col),
                   rows_out,
                   pl.BlockSpec((N_HEADS * V_ROWS, BLOCK), col),
                   rows_out,
                   pl.BlockSpec((LANES, BLOCK), col)],
        out_shape=[rows_sd,
                   jax.ShapeDtypeStruct((D_ATTN, n), BF16),
                   rows_sd,
                   jax.ShapeDtypeStruct((N_HEADS * V_ROWS, n), BF16),
                   rows_sd,
                   jax.ShapeDtypeStruct((LANES, n), BF16)],
        scratch_shapes=[pltpu.VMEM((nblk, D_ATTN), F32),
                        pltpu.VMEM((SUBLANES, CONV_ROWS, D_CONV), F32),
                        pltpu.VMEM((BLOCK, D_CONV), F32)],
        compiler_params=pltpu.CompilerParams(dimension_semantics=("arbitrary", "arbitrary"),
                                             vmem_limit_bytes=VMEM_LIMIT),
        name="inproj",
    )(x, mod3, g_norm, cos_t, sin_t, w_hi,
      jnp.broadcast_to(w_dw[:, None, :], (CONV_WIDTH, SUBLANES, D_CONV)),
      b_dw.reshape(1, -1), g_ln.reshape(1, -1), b_ln.reshape(1, -1), w_pw_bf, b_pw.reshape(1, -1))


def _attn_body(qt_ref, bias_ref, k_ref, vt_ref, sga_ref, onehot_ref, x_ref, mod_ref, yc_ref, wout_ref, gf_ref,
               o_ref, qa_ref, s_ref, smax_ref, m_ref, l_ref, acc_ref):
    i = pl.program_id(1)
    zeros_q = jnp.zeros((HEAD_DIM, BLOCK), BF16)
    for h in range(N_HEADS):
        qh = qt_ref[h * HEAD_DIM:(h + 1) * HEAD_DIM, :]
        bh = bias_ref[h * SEL_LANES:(h + 1) * SEL_LANES, :]
        pieces = [qh, zeros_q] if h % 2 == 0 else [zeros_q, qh]
        if h > 0:
            pieces.append(jnp.zeros((h * SEL_LANES, BLOCK), BF16))
        pieces.append(bh)
        if h < N_HEADS - 1:
            pieces.append(jnp.zeros(((N_HEADS - 1 - h) * SEL_LANES, BLOCK), BF16))
        qa_ref[h] = jnp.concatenate(pieces, axis=0)

    def score_fn(j, own):
        start = pl.multiple_of(j * BLOCK, BLOCK)
        sel = jnp.zeros((BLOCK, LANES), BF16) if own else onehot_ref[j]
        operands = {}

        def f(h):
            pair = h // 2
            if pair not in operands:
                kj = k_ref[pl.ds(start, BLOCK), pair * LANES:(pair + 1) * LANES]
                operands[pair] = jnp.concatenate([kj, sel], axis=1)
            return _dot(operands[pair], qa_ref[h])
        return f

    def head_step(j, h, st, m_cur, first):
        start = pl.multiple_of(j * BLOCK, BLOCK)
        if first:
            m_new = m_cur
        else:
            m_prev = m_ref[h:h + 1, :]
            m_new = jnp.maximum(m_prev, m_cur)
            alpha = jnp.exp2(m_prev - m_new)
        pt = jnp.exp2(st - m_new)
        vth = vt_ref[h * V_ROWS:(h + 1) * V_ROWS, pl.ds(start, BLOCK)]
        pv_l = _dot(vth, pt.astype(BF16))
        pv = pv_l[0:HEAD_DIM]
        l_cur = pv_l[HEAD_DIM:HEAD_DIM + 1]
        rows = slice(h * HEAD_DIM, (h + 1) * HEAD_DIM)
        acc_ref[rows, :] = pv if first else alpha * acc_ref[rows, :] + pv
        l_ref[h:h + 1, :] = l_cur if first else alpha * l_ref[h:h + 1, :] + l_cur
        m_ref[h:h + 1, :] = m_new

    key_idx = lax.broadcasted_iota(jnp.int32, (BLOCK, BLOCK), 0)
    qry_idx = lax.broadcasted_iota(jnp.int32, (BLOCK, BLOCK), 1)
    causal = key_idx <= qry_idx
    own_fn = score_fn(i, True)
    own = [own_fn(h) for h in range(N_HEADS)]

    def overlapped(slot_next, j_next, j_cur, cur, first):
        nxt = score_fn(j_next, False)
        for h in range(N_HEADS):
            st_next = nxt(h)
            s_ref[slot_next, h] = st_next
            smax_ref[slot_next, h:h + 1, :] = jnp.max(st_next, axis=0, keepdims=True)
            head_step(j_cur, h, *cur(h), first)

    odd = i % 2
    def own_scores(h):
        st = jnp.where(causal, own[h], NEG)
        return st, jnp.max(st, axis=0, keepdims=True)

    def stored(slot):
        return lambda h: (s_ref[slot, h], smax_ref[slot, h:h + 1, :])

    overlapped(odd, 0, i, own_scores, True)

    @pl.when(odd == 1)
    def _():
        overlapped(0, 1, 0, stored(1), False)

    def run_blocks(j0, count):
        for d in range(0, count, 2):
            overlapped(1, j0 + d + 1, j0 + d, stored(0), False)
            overlapped(0, jnp.minimum(j0 + d + 2, i), j0 + d + 1, stored(1), False)

    pairs = i // 2

    @pl.when(pairs % 2 == 1)
    def _():
        run_blocks(odd, 2)

    def quad_body(t, carry):
        run_blocks(odd + 2 * (pairs % 2) + ATTN_UNROLL * t, ATTN_UNROLL)
        return carry

    lax.fori_loop(0, pairs // 2, quad_body, 0)

    for half in range(2):
        qs = slice(half * LANES, (half + 1) * LANES)
        outs = [acc_ref[h * HEAD_DIM:(h + 1) * HEAD_DIM, qs] * (1.0 / l_ref[h:h + 1, qs]) for h in range(N_HEADS)]
        out = jnp.concatenate(outs, axis=0).T
        y_attn = (out * sga_ref[qs, :].astype(F32)).astype(BF16)
        y = _dot(yc_ref[qs, :], wout_ref[0:D_CONV, :]) + _dot(y_attn, wout_ref[D_CONV:, :])
        z = x_ref[0, qs, :] + mod_ref[0, 2:3, :] * y
        r = lax.rsqrt(jnp.mean(z * z, axis=-1, keepdims=True) + EPS)
        o_ref[0, qs, :] = z * r * gf_ref[...]


def _attention_outproj(qt, bias_t, k, vt, sga, x, mod3, y_conv, w_out_bf, g_final):
    bsz, seq, _ = x.shape
    nblk = seq // BLOCK
    assert nblk == SEL_LANES, "one selection lane per key block"
    row = lambda b, i: (b * nblk + i, 0)
    col = lambda b, i: (0, b * nblk + i)
    const = lambda b, i: (0, 0)
    xspec = pl.BlockSpec((1, BLOCK, D_MODEL), lambda b, i: (b, i, 0))
    lane = lax.broadcasted_iota(jnp.int32, (nblk, BLOCK, LANES), 2)
    blk = lax.broadcasted_iota(jnp.int32, (nblk, BLOCK, LANES), 0)
    onehot = ((lane % SEL_LANES) == blk).astype(BF16)
    return pl.pallas_call(
        _attn_body,
        grid=(bsz, nblk),
        in_specs=[pl.BlockSpec((D_ATTN, BLOCK), col),
                  pl.BlockSpec((LANES, BLOCK), col),
                  pl.BlockSpec((seq, D_ATTN), lambda b, i: (b, 0)),
                  pl.BlockSpec((N_HEADS * V_ROWS, seq), lambda b, i: (0, b)),
                  pl.BlockSpec((BLOCK, D_ATTN), row),
                  pl.BlockSpec((nblk, BLOCK, LANES), lambda b, i: (0, 0, 0)),
                  xspec,
                  pl.BlockSpec((1, 3, D_MODEL), lambda b, i: (b, 0, 0)),
                  pl.BlockSpec((BLOCK, D_CONV), row),
                  pl.BlockSpec((D_MODEL, D_MODEL), const),
                  pl.BlockSpec((1, D_MODEL), const)],
        out_specs=xspec,
        out_shape=jax.ShapeDtypeStruct(x.shape, F32),
        scratch_shapes=[pltpu.VMEM((N_HEADS, 2 * LANES, BLOCK), BF16),
                        pltpu.VMEM((2, N_HEADS, BLOCK, BLOCK), F32),
                        pltpu.VMEM((2, N_HEADS, BLOCK), F32),
                        pltpu.VMEM((N_HEADS, BLOCK), F32),
                        pltpu.VMEM((N_HEADS, BLOCK), F32),
                        pltpu.VMEM((D_ATTN, BLOCK), F32)],
        compiler_params=pltpu.CompilerParams(dimension_semantics=("arbitrary", "arbitrary"),
                                             vmem_limit_bytes=VMEM_LIMIT),
        name="attention_outproj",
    )(qt, bias_t, k, vt, sga, onehot, x, mod3, y_conv, w_out_bf, g_final.reshape(1, -1))


def kernel(x, c, positions, w_ada, b_ada, g_norm, w_in, w_dw, b_dw, g_ln_conv, b_ln_conv, w_pw, b_pw,
           w_out, g_final):
    bsz, seq, _ = x.shape
    assert w_ada.shape[0] == 1, "single-layer block: the final rmsnorm is fused into the last kernel"
    cos_t, sin_t = _rope_tables(positions)
    mod3 = _adaln(c, w_ada[0], b_ada[0]).reshape(bsz, 3, D_MODEL)
    w_hi = w_in[0].astype(BF16)
    y_conv, qt, k, vt, sga, bias_t = _inproj(x, mod3, g_norm[0].reshape(1, -1), cos_t, sin_t, w_hi,
                                             w_dw[0], b_dw[0], g_ln_conv[0], b_ln_conv[0],
                                             w_pw[0].astype(BF16), b_pw[0])
    return _attention_outproj(qt, bias_t, k, vt, sga, x, mod3, y_conv, w_out[0].astype(BF16), g_final)
```

```python
import jax
import jax.numpy as jnp
from jax import lax
from jax.experimental import pallas as pl
from jax.experimental.pallas import tpu as pltpu

F32 = jnp.float32
BF16 = jnp.bfloat16

D_MODEL = 1024
D_CONV = 512
D_ATTN = 512
HEAD_DIM = 64
N_HEADS = 8
CONV_WIDTH = 31
BLOCK = 256
TOP_K = 3
ROPE_DIM = 16
ROPE_HALF = 8
ROPE_THETA = 500000.0
EPS = 1e-6
NEG = -1e30
Q_SCALE = HEAD_DIM ** -0.5 * 1.4426950408889634
D_IN = 3584
C_VAL, C_GLU, C_GATE, C_Q, C_K, C_V, C_AG = 0, 512, 1024, 1536, 2048, 2560, 3072

LANES = 128
SUBLANES = 8
SEL_LANES = LANES // N_HEADS
V_ROWS = HEAD_DIM + 16
HALO = 32
CONV_ROWS = BLOCK + HALO
CONV_CHUNK = 32
ATTN_UNROLL = 4
VMEM_LIMIT = 48 * 1024 * 1024


def _split_bf16(a):
    hi = a.astype(BF16)
    lo = (a - hi.astype(F32)).astype(BF16)
    return hi, lo


def _dot(a, b):
    return jnp.dot(a, b, preferred_element_type=F32)


def _dot3(a_hi, a_lo, b_hi, b_lo):
    return _dot(a_hi, b_hi) + (_dot(a_hi, b_lo) + _dot(a_lo, b_hi))


def _sigmoid(a):
    return 1.0 / (1.0 + jnp.exp(-a))


def _silu(a):
    return a * _sigmoid(a)


def _adaln_body(c_ref, w_ref, b_ref, o_ref):
    a_hi, a_lo = _split_bf16(_silu(c_ref[...]))
    w_hi, w_lo = _split_bf16(w_ref[...])
    o_ref[...] = _dot3(a_hi, a_lo, w_hi, w_lo) + b_ref[...]


def _adaln(c, w_ada, b_ada):
    bsz = c.shape[0]
    n_out = w_ada.shape[1]
    tn = 1024
    return pl.pallas_call(
        _adaln_body,
        grid=(n_out // tn,),
        in_specs=[pl.BlockSpec((bsz, D_MODEL), lambda j: (0, 0)),
                  pl.BlockSpec((D_MODEL, tn), lambda j: (0, j)),
                  pl.BlockSpec((1, tn), lambda j: (0, j))],
        out_specs=pl.BlockSpec((bsz, tn), lambda j: (0, j)),
        out_shape=jax.ShapeDtypeStruct((bsz, n_out), F32),
        compiler_params=pltpu.CompilerParams(dimension_semantics=("arbitrary",),
                                             vmem_limit_bytes=VMEM_LIMIT),
        name="adaln",
    )(c, w_ada, b_ada.reshape(1, n_out))


def _rope_table_body(inv_ref, pos_ref, cos_ref, sin_ref):
    pos = pos_ref[...].astype(F32)
    for i in range(ROPE_HALF):
        ang = pos * inv_ref[i]
        cos_ref[i] = jnp.cos(ang)
        sin_ref[i] = jnp.sin(ang)


def _rope_tables(positions):
    bsz, seq = positions.shape
    inv = ROPE_THETA ** (-(jnp.arange(ROPE_HALF, dtype=F32) * 2.0) / ROPE_DIM)
    cos, sin = pl.pallas_call(
        _rope_table_body,
        in_specs=[pl.BlockSpec(memory_space=pltpu.SMEM),
                  pl.BlockSpec((bsz, seq), lambda: (0, 0))],
        out_specs=[pl.BlockSpec((ROPE_HALF, bsz, seq), lambda: (0, 0, 0))] * 2,
        out_shape=[jax.ShapeDtypeStruct((ROPE_HALF, bsz, seq), F32)] * 2,
        name="rope_table",
    )(inv, positions)
    return cos.reshape(ROPE_HALF, bsz * seq), sin.reshape(ROPE_HALF, bsz * seq)


def _rope_t(t, cos, sin):
    pieces = []
    for hd in range(N_HEADS):
        r0 = hd * HEAD_DIM
        t1 = t[r0:r0 + ROPE_HALF]
        t2 = t[r0 + ROPE_HALF:r0 + ROPE_DIM]
        pieces += [t1 * cos - t2 * sin, t2 * cos + t1 * sin, t[r0 + ROPE_DIM:r0 + HEAD_DIM]]
    return jnp.concatenate(pieces, axis=0)


def _conv_steps(u, win_ref, conv_ref, wdw_ref):
    def setup():
        win_ref[0, HALO:, :] = u
        for s in range(1, SUBLANES):
            win_ref[s, 0:CONV_ROWS - SUBLANES, :] = win_ref[0, s:s + CONV_ROWS - SUBLANES, :]

    def chunk(c, after):
        r0 = c * CONV_CHUNK
        base = HALO - (CONV_WIDTH - 1)
        zero = jnp.minimum(jnp.maximum(after[0:CONV_CHUNK, :], 0.0), 0.0)
        acc = jnp.concatenate([zero] * (D_CONV // zero.shape[1]), axis=1)
        for w in range(CONV_WIDTH):
            a, s = divmod(base + w, SUBLANES)
            rows = slice(r0 + a * SUBLANES, r0 + a * SUBLANES + CONV_CHUNK)
            tap = jnp.concatenate([wdw_ref[w]] * (CONV_CHUNK // SUBLANES), axis=0)
            acc = acc + win_ref[s, rows, :] * tap
        conv_ref[r0:r0 + CONV_CHUNK, :] = acc

    return [lambda after: setup()] + [lambda after, c=c: chunk(c, after) for c in range(BLOCK // CONV_CHUNK)]


def _conv_finish(gate_act, win_ref, conv_ref, bdw_ref, gln_ref, bln_ref, wpw_ref, bpw_ref):
    win_ref[0, 0:HALO, :] = win_ref[0, BLOCK:BLOCK + HALO, :]
    y = conv_ref[...] + bdw_ref[...]
    mu = jnp.mean(y, axis=-1, keepdims=True)
    yc = y - mu
    var = jnp.mean(yc * yc, axis=-1, keepdims=True)
    z = _silu(yc * lax.rsqrt(var + EPS) * gln_ref[...] + bln_ref[...])
    return (_dot(z.astype(BF16), wpw_ref[...]) + bpw_ref[...]) * gate_act


def _inproj_body(x_ref, mod_ref, g_ref, cos_ref, sin_ref, whi_ref,
                 wdw_ref, bdw_ref, gln_ref, bln_ref, wpw_ref, bpw_ref,
                 yc_ref, qt_ref, k_ref, vt_ref, sga_ref, bias_ref,
                 km_ref, win_ref, conv_ref):
    i = pl.program_id(1)

    @pl.when(i == 0)
    def _():
        km_ref[...] = jnp.zeros_like(km_ref)
        win_ref[0, 0:HALO, :] = jnp.zeros((HALO, D_CONV), F32)

    x = x_ref[0]
    r = lax.rsqrt(jnp.mean(x * x, axis=-1, keepdims=True) + EPS)
    shift = mod_ref[0, 0:1, :]
    scale = mod_ref[0, 1:2, :]
    h = (x * r) * (g_ref[...] * (1.0 + scale)) + shift
    h_hi = h.astype(BF16)

    def w_hi(c0):
        return whi_ref[:, c0:c0 + 512]

    u = _dot(h_hi, w_hi(C_VAL)) * _sigmoid(_dot(h_hi, w_hi(C_GLU)))
    plan = [(lambda: _dot(h_hi, w_hi(C_GATE)), 2), (lambda: _dot(h_hi, w_hi(C_V)), 2),
            (lambda: _dot(h_hi, w_hi(C_AG)), 2), (lambda: _dot(h_hi, w_hi(C_Q)), 2),
            (lambda: _dot(h_hi, w_hi(C_K)), 1)]
    conv_steps = _conv_steps(u, win_ref, conv_ref, wdw_ref)
    assert sum(n for _, n in plan) == len(conv_steps)
    products = []
    for matmul, n_steps in plan:
        products.append(matmul())
        for _ in range(n_steps):
            conv_steps.pop(0)(products[-1])
    c_gate, v, a_gate, q, k = products
    y_conv = _conv_finish(_silu(c_gate), win_ref, conv_ref, bdw_ref, gln_ref, bln_ref, wpw_ref, bpw_ref)
    yc_ref[...] = y_conv.astype(BF16)

    vt = v.T.astype(BF16)
    ones = jnp.ones((V_ROWS - HEAD_DIM, BLOCK), BF16)
    vt_ref[...] = jnp.concatenate(
        [piece for hd in range(N_HEADS) for piece in (vt[hd * HEAD_DIM:(hd + 1) * HEAD_DIM], ones)], axis=0)
    sga_ref[...] = _silu(a_gate).astype(BF16)

    cos = cos_ref[...]
    sin = sin_ref[...]
    qt = _rope_t(q.T, cos, sin) * Q_SCALE
    kr = _rope_t(k.T, cos, sin).T
    qt_hi = qt.astype(BF16)
    qt_ref[...] = qt_hi
    k_ref[...] = kr.astype(BF16)

    km = km_ref[...]
    lane_k = lax.broadcasted_iota(jnp.int32, km.shape, 1)
    kmt = jnp.concatenate([jnp.where((lane_k // HEAD_DIM) == hd, km, 0.0) for hd in range(N_HEADS)], axis=0)
    gt = _dot(kmt.astype(BF16), qt_hi)

    nb = km.shape[0]
    n_iota = lax.broadcasted_iota(jnp.int32, (nb, BLOCK), 0)
    past = n_iota < i
    n_sel = jnp.minimum(i, TOP_K)
    biases = []
    for hd in range(N_HEADS):
        g = jnp.where(past, gt[hd * nb:(hd + 1) * nb, :], NEG)
        rank = jnp.zeros((nb, BLOCK), jnp.int32)
        for n2 in range(nb):
            row = g[n2:n2 + 1, :]
            beats = (row > g) | ((row == g) & (n_iota > n2))
            rank = rank + jnp.where(beats, 1, 0)
        biases.append(jnp.where(rank < n_sel, 0.0, NEG))
    bias_ref[...] = jnp.concatenate(biases, axis=0).astype(BF16)

    km_ref[pl.ds(i, 1), :] = jnp.mean(kr, axis=0, keepdims=True)


def _inproj(x, mod3, g_norm, cos_t, sin_t, w_hi, w_dw, b_dw, g_ln, b_ln, w_pw_bf, b_pw):
    bsz, seq, _ = x.shape
    nblk = seq // BLOCK
    n = bsz * seq
    row = lambda b, i: (b * nblk + i, 0)
    col = lambda b, i: (0, b * nblk + i)
    const = lambda b, i: (0, 0)
    rows_out = pl.BlockSpec((BLOCK, 512), row)
    rows_sd = jax.ShapeDtypeStruct((n, 512), BF16)
    vec = pl.BlockSpec((1, D_CONV), const)
    return pl.pallas_call(
        _inproj_body,
        grid=(bsz, nblk),
        in_specs=[pl.BlockSpec((1, BLOCK, D_MODEL), lambda b, i: (b, i, 0)),
                  pl.BlockSpec((1, 3, D_MODEL), lambda b, i: (b, 0, 0)),
                  pl.BlockSpec((1, D_MODEL), const),
                  pl.BlockSpec((ROPE_HALF, BLOCK), col),
                  pl.BlockSpec((ROPE_HALF, BLOCK), col),
                  pl.BlockSpec((D_MODEL, D_IN), const),
                  pl.BlockSpec((CONV_WIDTH, SUBLANES, D_CONV), lambda b, i: (0, 0, 0)),
                  vec, vec, vec,
                  pl.BlockSpec((D_CONV, D_CONV), const),
                  vec],
        out_specs=[rows_out,
                   pl.BlockSpec((D_ATTN, BLOCK), col),
                   rows_out,
                   pl.BlockSpec((N_HEADS * V_ROWS, BLOCK), col),
                   rows_out,
                   pl.BlockSpec((LANES, BLOCK), col)],
        out_shape=[rows_sd,
                   jax.ShapeDtypeStruct((D_ATTN, n), BF16),
                   rows_sd,
                   jax.ShapeDtypeStruct((N_HEADS * V_ROWS, n), BF16),
                   rows_sd,
                   jax.ShapeDtypeStruct((LANES, n), BF16)],
        scratch_shapes=[pltpu.VMEM((nblk, D_ATTN), F32),
                        pltpu.VMEM((SUBLANES, CONV_ROWS, D_CONV), F32),
                        pltpu.VMEM((BLOCK, D_CONV), F32)],
        compiler_params=pltpu.CompilerParams(dimension_semantics=("arbitrary", "arbitrary"),
                                             vmem_limit_bytes=VMEM_LIMIT),
        name="inproj",
    )(x, mod3, g_norm, cos_t, sin_t, w_hi,
      jnp.broadcast_to(w_dw[:, None, :], (CONV_WIDTH, SUBLANES, D_CONV)),
      b_dw.reshape(1, -1), g_ln.reshape(1, -1), b_ln.reshape(1, -1), w_pw_bf, b_pw.reshape(1, -1))


def _attn_body(qt_ref, bias_ref, k_ref, vt_ref, sga_ref, onehot_ref, x_ref, mod_ref, yc_ref, wout_ref, gf_ref,
               o_ref, qa_ref, s_ref, smax_ref, m_ref, l_ref, acc_ref):
    i = pl.program_id(1)
    y_out_conv = _dot(yc_ref[...], wout_ref[0:D_CONV, :])
    zeros_q = jnp.zeros((HEAD_DIM, BLOCK), BF16)
    for h in range(N_HEADS):
        qh = qt_ref[h * HEAD_DIM:(h + 1) * HEAD_DIM, :]
        bh = bias_ref[h * SEL_LANES:(h + 1) * SEL_LANES, :]
        pieces = [qh, zeros_q] if h % 2 == 0 else [zeros_q, qh]
        if h > 0:
            pieces.append(jnp.zeros((h * SEL_LANES, BLOCK), BF16))
        pieces.append(bh)
        if h < N_HEADS - 1:
            pieces.append(jnp.zeros(((N_HEADS - 1 - h) * SEL_LANES, BLOCK), BF16))
        qa_ref[h] = jnp.concatenate(pieces, axis=0)

    def score_fn(j, own):
        start = pl.multiple_of(j * BLOCK, BLOCK)
        sel = jnp.zeros((BLOCK, LANES), BF16) if own else onehot_ref[j]
        operands = {}

        def f(h):
            pair = h // 2
            if pair not in operands:
                kj = k_ref[pl.ds(start, BLOCK), pair * LANES:(pair + 1) * LANES]
                operands[pair] = jnp.concatenate([kj, sel], axis=1)
            return _dot(operands[pair], qa_ref[h])
        return f

    def head_step(j, h, st, m_cur, first):
        start = pl.multiple_of(j * BLOCK, BLOCK)
        if first:
            m_new = m_cur
        else:
            m_prev = m_ref[h:h + 1, :]
            m_new = jnp.maximum(m_prev, m_cur)
            alpha = jnp.exp2(m_prev - m_new)
        pt = jnp.exp2(st - m_new)
        vth = vt_ref[h * V_ROWS:(h + 1) * V_ROWS, pl.ds(start, BLOCK)]
        pv_l = _dot(vth, pt.astype(BF16))
        pv = pv_l[0:HEAD_DIM]
        l_cur = pv_l[HEAD_DIM:HEAD_DIM + 1]
        rows = slice(h * HEAD_DIM, (h + 1) * HEAD_DIM)
        acc_ref[rows, :] = pv if first else alpha * acc_ref[rows, :] + pv
        l_ref[h:h + 1, :] = l_cur if first else alpha * l_ref[h:h + 1, :] + l_cur
        m_ref[h:h + 1, :] = m_new

    key_idx = lax.broadcasted_iota(jnp.int32, (BLOCK, BLOCK), 0)
    qry_idx = lax.broadcasted_iota(jnp.int32, (BLOCK, BLOCK), 1)
    causal = key_idx <= qry_idx
    own_fn = score_fn(i, True)
    own = [own_fn(h) for h in range(N_HEADS)]

    def overlapped(slot_next, j_next, j_cur, cur, first):
        nxt = score_fn(j_next, False)
        for h in range(N_HEADS):
            st_next = nxt(h)
            s_ref[slot_next, h] = st_next
            smax_ref[slot_next, h:h + 1, :] = jnp.max(st_next, axis=0, keepdims=True)
            head_step(j_cur, h, *cur(h), first)

    odd = i % 2
    def own_scores(h):
        st = jnp.where(causal, own[h], NEG)
        return st, jnp.max(st, axis=0, keepdims=True)

    def stored(slot):
        return lambda h: (s_ref[slot, h], smax_ref[slot, h:h + 1, :])

    overlapped(odd, 0, i, own_scores, True)

    @pl.when(odd == 1)
    def _():
        overlapped(0, 1, 0, stored(1), False)

    def run_blocks(j0, count):
        for d in range(0, count, 2):
            overlapped(1, j0 + d + 1, j0 + d, stored(0), False)
            overlapped(0, jnp.minimum(j0 + d + 2, i), j0 + d + 1, stored(1), False)

    pairs = i // 2

    @pl.when(pairs % 2 == 1)
    def _():
        run_blocks(odd, 2)

    def quad_body(t, carry):
        run_blocks(odd + 2 * (pairs % 2) + ATTN_UNROLL * t, ATTN_UNROLL)
        return carry

    lax.fori_loop(0, pairs // 2, quad_body, 0)

    outs = [acc_ref[h * HEAD_DIM:(h + 1) * HEAD_DIM, :] * (1.0 / l_ref[h:h + 1, :]) for h in range(N_HEADS)]
    out = jnp.concatenate(outs, axis=0).T
    y_attn = (out * sga_ref[...].astype(F32)).astype(BF16)

    y = y_out_conv + _dot(y_attn, wout_ref[D_CONV:, :])
    z = x_ref[0] + mod_ref[0, 2:3, :] * y
    r = lax.rsqrt(jnp.mean(z * z, axis=-1, keepdims=True) + EPS)
    o_ref[0] = z * r * gf_ref[...]


def _attention_outproj(qt, bias_t, k, vt, sga, x, mod3, y_conv, w_out_bf, g_final):
    bsz, seq, _ = x.shape
    nblk = seq // BLOCK
    assert nblk == SEL_LANES, "one selection lane per key block"
    row = lambda b, i: (b * nblk + i, 0)
    col = lambda b, i: (0, b * nblk + i)
    const = lambda b, i: (0, 0)
    xspec = pl.BlockSpec((1, BLOCK, D_MODEL), lambda b, i: (b, i, 0))
    lane = lax.broadcasted_iota(jnp.int32, (nblk, BLOCK, LANES), 2)
    blk = lax.broadcasted_iota(jnp.int32, (nblk, BLOCK, LANES), 0)
    onehot = ((lane % SEL_LANES) == blk).astype(BF16)
    return pl.pallas_call(
        _attn_body,
        grid=(bsz, nblk),
        in_specs=[pl.BlockSpec((D_ATTN, BLOCK), col),
                  pl.BlockSpec((LANES, BLOCK), col),
                  pl.BlockSpec((seq, D_ATTN), lambda b, i: (b, 0)),
                  pl.BlockSpec((N_HEADS * V_ROWS, seq), lambda b, i: (0, b)),
                  pl.BlockSpec((BLOCK, D_ATTN), row),
                  pl.BlockSpec((nblk, BLOCK, LANES), lambda b, i: (0, 0, 0)),
                  xspec,
                  pl.BlockSpec((1, 3, D_MODEL), lambda b, i: (b, 0, 0)),
                  pl.BlockSpec((BLOCK, D_CONV), row),
                  pl.BlockSpec((D_MODEL, D_MODEL), const),
                  pl.BlockSpec((1, D_MODEL), const)],
        out_specs=xspec,
        out_shape=jax.ShapeDtypeStruct(x.shape, F32),
        scratch_shapes=[pltpu.VMEM((N_HEADS, 2 * LANES, BLOCK), BF16),
                        pltpu.VMEM((2, N_HEADS, BLOCK, BLOCK), F32),
                        pltpu.VMEM((2, N_HEADS, BLOCK), F32),
                        pltpu.VMEM((N_HEADS, BLOCK), F32),
                        pltpu.VMEM((N_HEADS, BLOCK), F32),
                        pltpu.VMEM((D_ATTN, BLOCK), F32)],
        compiler_params=pltpu.CompilerParams(dimension_semantics=("arbitrary", "arbitrary"),
                                             vmem_limit_bytes=VMEM_LIMIT),
        name="attention_outproj",
    )(qt, bias_t, k, vt, sga, onehot, x, mod3, y_conv, w_out_bf, g_final.reshape(1, -1))


def kernel(x, c, positions, w_ada, b_ada, g_norm, w_in, w_dw, b_dw, g_ln_conv, b_ln_conv, w_pw, b_pw,
           w_out, g_final):
    bsz, seq, _ = x.shape
    assert w_ada.shape[0] == 1, "single-layer block: the final rmsnorm is fused into the last kernel"
    cos_t, sin_t = _rope_tables(positions)
    mod3 = _adaln(c, w_ada[0], b_ada[0]).reshape(bsz, 3, D_MODEL)
    w_hi = w_in[0].astype(BF16)
    y_conv, qt, k, vt, sga, bias_t = _inproj(x, mod3, g_norm[0].reshape(1, -1), cos_t, sin_t, w_hi,
                                             w_dw[0], b_dw[0], g_ln_conv[0], b_ln_conv[0],
                                             w_pw[0].astype(BF16), b_pw[0])
    return _attention_outproj(qt, bias_t, k, vt, sga, x, mod3, y_conv, w_out[0].astype(BF16), g_final)
```

```python
import jax
import jax.numpy as jnp
from jax import lax
from jax.experimental import pallas as pl
from jax.experimental.pallas import tpu as pltpu

F32 = jnp.float32
BF16 = jnp.bfloat16

D_MODEL = 1024
D_CONV = 512
D_ATTN = 512
HEAD_DIM = 64
N_HEADS = 8
CONV_WIDTH = 31
BLOCK = 256
TOP_K = 3
ROPE_DIM = 16
ROPE_HALF = 8
ROPE_THETA = 500000.0
EPS = 1e-6
NEG = -1e30
Q_SCALE = HEAD_DIM ** -0.5 * 1.4426950408889634
D_IN = 3584
C_VAL, C_GLU, C_GATE, C_Q, C_K, C_V, C_AG = 0, 512, 1024, 1536, 2048, 2560, 3072

LANES = 128
SUBLANES = 8
SEL_LANES = LANES // N_HEADS
V_ROWS = HEAD_DIM + 16
HALO = 32
CONV_ROWS = BLOCK + HALO
CONV_CHUNK = 32
ATTN_UNROLL = 8
VMEM_LIMIT = 48 * 1024 * 1024


def _split_bf16(a):
    hi = a.astype(BF16)
    lo = (a - hi.astype(F32)).astype(BF16)
    return hi, lo


def _dot(a, b):
    return jnp.dot(a, b, preferred_element_type=F32)


def _dot3(a_hi, a_lo, b_hi, b_lo):
    return _dot(a_hi, b_hi) + (_dot(a_hi, b_lo) + _dot(a_lo, b_hi))


def _sigmoid(a):
    return 1.0 / (1.0 + jnp.exp(-a))


def _silu(a):
    return a * _sigmoid(a)


def _adaln_body(c_ref, w_ref, b_ref, o_ref):
    a_hi, a_lo = _split_bf16(_silu(c_ref[...]))
    w_hi, w_lo = _split_bf16(w_ref[...])
    o_ref[...] = _dot3(a_hi, a_lo, w_hi, w_lo) + b_ref[...]


def _adaln(c, w_ada, b_ada):
    bsz = c.shape[0]
    n_out = w_ada.shape[1]
    tn = 1024
    return pl.pallas_call(
        _adaln_body,
        grid=(n_out // tn,),
        in_specs=[pl.BlockSpec((bsz, D_MODEL), lambda j: (0, 0)),
                  pl.BlockSpec((D_MODEL, tn), lambda j: (0, j)),
                  pl.BlockSpec((1, tn), lambda j: (0, j))],
        out_specs=pl.BlockSpec((bsz, tn), lambda j: (0, j)),
        out_shape=jax.ShapeDtypeStruct((bsz, n_out), F32),
        compiler_params=pltpu.CompilerParams(dimension_semantics=("arbitrary",),
                                             vmem_limit_bytes=VMEM_LIMIT),
        name="adaln",
    )(c, w_ada, b_ada.reshape(1, n_out))


def _rope_table_body(inv_ref, pos_ref, cos_ref, sin_ref):
    pos = pos_ref[...].astype(F32)
    for i in range(ROPE_HALF):
        ang = pos * inv_ref[i]
        cos_ref[i] = jnp.cos(ang)
        sin_ref[i] = jnp.sin(ang)


def _rope_tables(positions):
    bsz, seq = positions.shape
    inv = ROPE_THETA ** (-(jnp.arange(ROPE_HALF, dtype=F32) * 2.0) / ROPE_DIM)
    cos, sin = pl.pallas_call(
        _rope_table_body,
        in_specs=[pl.BlockSpec(memory_space=pltpu.SMEM),
                  pl.BlockSpec((bsz, seq), lambda: (0, 0))],
        out_specs=[pl.BlockSpec((ROPE_HALF, bsz, seq), lambda: (0, 0, 0))] * 2,
        out_shape=[jax.ShapeDtypeStruct((ROPE_HALF, bsz, seq), F32)] * 2,
        name="rope_table",
    )(inv, positions)
    return cos.reshape(ROPE_HALF, bsz * seq), sin.reshape(ROPE_HALF, bsz * seq)


def _rope_t(t, cos, sin):
    pieces = []
    for hd in range(N_HEADS):
        r0 = hd * HEAD_DIM
        t1 = t[r0:r0 + ROPE_HALF]
        t2 = t[r0 + ROPE_HALF:r0 + ROPE_DIM]
        pieces += [t1 * cos - t2 * sin, t2 * cos + t1 * sin, t[r0 + ROPE_DIM:r0 + HEAD_DIM]]
    return jnp.concatenate(pieces, axis=0)


def _conv_steps(u, win_ref, conv_ref, wdw_ref):
    def setup():
        win_ref[0, HALO:, :] = u
        for s in range(1, SUBLANES):
            win_ref[s, 0:CONV_ROWS - SUBLANES, :] = win_ref[0, s:s + CONV_ROWS - SUBLANES, :]

    def chunk(c, after):
        r0 = c * CONV_CHUNK
        base = HALO - (CONV_WIDTH - 1)
        zero = jnp.minimum(jnp.maximum(after[0:CONV_CHUNK, :], 0.0), 0.0)
        acc = jnp.concatenate([zero] * (D_CONV // zero.shape[1]), axis=1)
        for w in range(CONV_WIDTH):
            a, s = divmod(base + w, SUBLANES)
            rows = slice(r0 + a * SUBLANES, r0 + a * SUBLANES + CONV_CHUNK)
            tap = jnp.concatenate([wdw_ref[w]] * (CONV_CHUNK // SUBLANES), axis=0)
            acc = acc + win_ref[s, rows, :] * tap
        conv_ref[r0:r0 + CONV_CHUNK, :] = acc

    return [lambda after: setup()] + [lambda after, c=c: chunk(c, after) for c in range(BLOCK // CONV_CHUNK)]


def _conv_finish(gate_act, win_ref, conv_ref, bdw_ref, gln_ref, bln_ref, wpw_ref, bpw_ref):
    win_ref[0, 0:HALO, :] = win_ref[0, BLOCK:BLOCK + HALO, :]
    y = conv_ref[...] + bdw_ref[...]
    mu = jnp.mean(y, axis=-1, keepdims=True)
    yc = y - mu
    var = jnp.mean(yc * yc, axis=-1, keepdims=True)
    z = _silu(yc * lax.rsqrt(var + EPS) * gln_ref[...] + bln_ref[...])
    return (_dot(z.astype(BF16), wpw_ref[...]) + bpw_ref[...]) * gate_act


def _inproj_body(x_ref, mod_ref, g_ref, cos_ref, sin_ref, whi_ref,
                 wdw_ref, bdw_ref, gln_ref, bln_ref, wpw_ref, bpw_ref,
                 yc_ref, qt_ref, k_ref, vt_ref, sga_ref, bias_ref,
                 km_ref, win_ref, conv_ref):
    i = pl.program_id(1)

    @pl.when(i == 0)
    def _():
        km_ref[...] = jnp.zeros_like(km_ref)
        win_ref[0, 0:HALO, :] = jnp.zeros((HALO, D_CONV), F32)

    x = x_ref[0]
    r = lax.rsqrt(jnp.mean(x * x, axis=-1, keepdims=True) + EPS)
    shift = mod_ref[0, 0:1, :]
    scale = mod_ref[0, 1:2, :]
    h = (x * r) * (g_ref[...] * (1.0 + scale)) + shift
    h_hi = h.astype(BF16)

    def w_hi(c0):
        return whi_ref[:, c0:c0 + 512]

    u = _dot(h_hi, w_hi(C_VAL)) * _sigmoid(_dot(h_hi, w_hi(C_GLU)))
    plan = [(lambda: _dot(h_hi, w_hi(C_GATE)), 1), (lambda: _dot(h_hi, w_hi(C_V)), 2),
            (lambda: _dot(h_hi, w_hi(C_AG)), 2), (lambda: _dot(h_hi, w_hi(C_Q)), 2),
            (lambda: _dot(h_hi, w_hi(C_K)), 2)]
    conv_steps = _conv_steps(u, win_ref, conv_ref, wdw_ref)
    assert sum(n for _, n in plan) == len(conv_steps)
    products = []
    for matmul, n_steps in plan:
        products.append(matmul())
        for _ in range(n_steps):
            conv_steps.pop(0)(products[-1])
    c_gate, v, a_gate, q, k = products
    y_conv = _conv_finish(_silu(c_gate), win_ref, conv_ref, bdw_ref, gln_ref, bln_ref, wpw_ref, bpw_ref)
    yc_ref[...] = y_conv.astype(BF16)

    vt = v.T.astype(BF16)
    ones = jnp.ones((V_ROWS - HEAD_DIM, BLOCK), BF16)
    vt_ref[...] = jnp.concatenate(
        [piece for hd in range(N_HEADS) for piece in (vt[hd * HEAD_DIM:(hd + 1) * HEAD_DIM], ones)], axis=0)
    sga_ref[...] = _silu(a_gate).astype(BF16)

    cos = cos_ref[...]
    sin = sin_ref[...]
    qt = _rope_t(q.T, cos, sin) * Q_SCALE
    kr = _rope_t(k.T, cos, sin).T
    qt_hi = qt.astype(BF16)
    qt_ref[...] = qt_hi
    k_ref[...] = kr.astype(BF16)

    km = km_ref[...]
    lane_k = lax.broadcasted_iota(jnp.int32, km.shape, 1)
    kmt = jnp.concatenate([jnp.where((lane_k // HEAD_DIM) == hd, km, 0.0) for hd in range(N_HEADS)], axis=0)
    gt = _dot(kmt.astype(BF16), qt_hi)

    nb = km.shape[0]
    n_iota = lax.broadcasted_iota(jnp.int32, (nb, BLOCK), 0)
    past = n_iota < i
    n_sel = jnp.minimum(i, TOP_K)
    biases = []
    for hd in range(N_HEADS):
        g = jnp.where(past, gt[hd * nb:(hd + 1) * nb, :], NEG)
        rank = jnp.zeros((nb, BLOCK), jnp.int32)
        for n2 in range(nb):
            row = g[n2:n2 + 1, :]
            beats = (row > g) | ((row == g) & (n_iota > n2))
            rank = rank + jnp.where(beats, 1, 0)
        biases.append(jnp.where(rank < n_sel, 0.0, NEG))
    bias_ref[...] = jnp.concatenate(biases, axis=0).astype(BF16)

    km_ref[pl.ds(i, 1), :] = jnp.mean(kr, axis=0, keepdims=True)


def _inproj(x, mod3, g_norm, cos_t, sin_t, w_hi, w_dw, b_dw, g_ln, b_ln, w_pw_bf, b_pw):
    bsz, seq, _ = x.shape
    nblk = seq // BLOCK
    n = bsz * seq
    row = lambda b, i: (b * nblk + i, 0)
    col = lambda b, i: (0, b * nblk + i)
    const = lambda b, i: (0, 0)
    rows_out = pl.BlockSpec((BLOCK, 512), row)
    rows_sd = jax.ShapeDtypeStruct((n, 512), BF16)
    vec = pl.BlockSpec((1, D_CONV), const)
    return pl.pallas_call(
        _inproj_body,
        grid=(bsz, nblk),
        in_specs=[pl.BlockSpec((1, BLOCK, D_MODEL), lambda b, i: (b, i, 0)),
                  pl.BlockSpec((1, 3, D_MODEL), lambda b, i: (b, 0, 0)),
                  pl.BlockSpec((1, D_MODEL), const),
                  pl.BlockSpec((ROPE_HALF, BLOCK), col),
                  pl.BlockSpec((ROPE_HALF, BLOCK), col),
                  pl.BlockSpec((D_MODEL, D_IN), const),
                  pl.BlockSpec((CONV_WIDTH, SUBLANES, D_CONV), lambda b, i: (0, 0, 0)),
                  vec, vec, vec,
                  pl.BlockSpec((D_CONV, D_CONV), const),
                  vec],
        out_specs=[rows_out,
                   pl.BlockSpec((D_ATTN, BLOCK), col),
                   rows_out,
                   pl.BlockSpec((N_HEADS * V_ROWS, BLOCK), col),
                   rows_out,
                   pl.BlockSpec((LANES, BLOCK), col)],
        out_shape=[rows_sd,
                   jax.ShapeDtypeStruct((D_ATTN, n), BF16),
                   rows_sd,
                   jax.ShapeDtypeStruct((N_HEADS * V_ROWS, n), BF16),
                   rows_sd,
                   jax.ShapeDtypeStruct((LANES, n), BF16)],
        scratch_shapes=[pltpu.VMEM((nblk, D_ATTN), F32),
                        pltpu.VMEM((SUBLANES, CONV_ROWS, D_CONV), F32),
                        pltpu.VMEM((BLOCK, D_CONV), F32)],
        compiler_params=pltpu.CompilerParams(dimension_semantics=("arbitrary", "arbitrary"),
                                             vmem_limit_bytes=VMEM_LIMIT),
        name="inproj",
    )(x, mod3, g_norm, cos_t, sin_t, w_hi,
      jnp.broadcast_to(w_dw[:, None, :], (CONV_WIDTH, SUBLANES, D_CONV)),
      b_dw.reshape(1, -1), g_ln.reshape(1, -1), b_ln.reshape(1, -1), w_pw_bf, b_pw.reshape(1, -1))


def _attn_body(qt_ref, bias_ref, k_ref, vt_ref, sga_ref, onehot_ref, x_ref, mod_ref, yc_ref, wout_ref, gf_ref,
               o_ref, qa_ref, s_ref, smax_ref, m_ref, l_ref, acc_ref):
    i = pl.program_id(1)
    y_out_conv = _dot(yc_ref[...], wout_ref[0:D_CONV, :])
    zeros_q = jnp.zeros((HEAD_DIM, BLOCK), BF16)
    for h in range(N_HEADS):
        qh = qt_ref[h * HEAD_DIM:(h + 1) * HEAD_DIM, :]
        bh = bias_ref[h * SEL_LANES:(h + 1) * SEL_LANES, :]
        pieces = [qh, zeros_q] if h % 2 == 0 else [zeros_q, qh]
        if h > 0:
            pieces.append(jnp.zeros((h * SEL_LANES, BLOCK), BF16))
        pieces.append(bh)
        if h < N_HEADS - 1:
            pieces.append(jnp.zeros(((N_HEADS - 1 - h) * SEL_LANES, BLOCK), BF16))
        qa_ref[h] = jnp.concatenate(pieces, axis=0)

    def score_fn(j, own):
        start = pl.multiple_of(j * BLOCK, BLOCK)
        sel = jnp.zeros((BLOCK, LANES), BF16) if own else onehot_ref[j]
        operands = {}

        def f(h):
            pair = h // 2
            if pair not in operands:
                kj = k_ref[pl.ds(start, BLOCK), pair * LANES:(pair + 1) * LANES]
                operands[pair] = jnp.concatenate([kj, sel], axis=1)
            return _dot(operands[pair], qa_ref[h])
        return f

    def head_step(j, h, st, m_cur, first):
        start = pl.multiple_of(j * BLOCK, BLOCK)
        if first:
            m_new = m_cur
        else:
            m_prev = m_ref[h:h + 1, :]
            m_new = jnp.maximum(m_prev, m_cur)
            alpha = jnp.exp2(m_prev - m_new)
        pt = jnp.exp2(st - m_new)
        vth = vt_ref[h * V_ROWS:(h + 1) * V_ROWS, pl.ds(start, BLOCK)]
        pv_l = _dot(vth, pt.astype(BF16))
        pv = pv_l[0:HEAD_DIM]
        l_cur = pv_l[HEAD_DIM:HEAD_DIM + 1]
        rows = slice(h * HEAD_DIM, (h + 1) * HEAD_DIM)
        acc_ref[rows, :] = pv if first else alpha * acc_ref[rows, :] + pv
        l_ref[h:h + 1, :] = l_cur if first else alpha * l_ref[h:h + 1, :] + l_cur
        m_ref[h:h + 1, :] = m_new

    key_idx = lax.broadcasted_iota(jnp.int32, (BLOCK, BLOCK), 0)
    qry_idx = lax.broadcasted_iota(jnp.int32, (BLOCK, BLOCK), 1)
    causal = key_idx <= qry_idx
    own_fn = score_fn(i, True)
    own = [own_fn(h) for h in range(N_HEADS)]

    def overlapped(slot_next, j_next, j_cur, cur, first):
        nxt = score_fn(j_next, False)
        for h in range(N_HEADS):
            st_next = nxt(h)
            s_ref[slot_next, h] = st_next
            smax_ref[slot_next, h:h + 1, :] = jnp.max(st_next, axis=0, keepdims=True)
            head_step(j_cur, h, *cur(h), first)

    odd = i % 2
    def own_scores(h):
        st = jnp.where(causal, own[h], NEG)
        return st, jnp.max(st, axis=0, keepdims=True)

    def stored(slot):
        return lambda h: (s_ref[slot, h], smax_ref[slot, h:h + 1, :])

    overlapped(odd, 0, i, own_scores, True)

    @pl.when(odd == 1)
    def _():
        overlapped(0, 1, 0, stored(1), False)

    def run_blocks(j0, count):
        for d in range(0, count, 2):
            overlapped(1, j0 + d + 1, j0 + d, stored(0), False)
            overlapped(0, jnp.minimum(j0 + d + 2, i), j0 + d + 1, stored(1), False)

    pairs = i // 2
    quads = pairs // 2

    @pl.when(pairs % 2 == 1)
    def _():
        run_blocks(odd, 2)

    @pl.when(quads % 2 == 1)
    def _():
        run_blocks(odd + 2 * (pairs % 2), 4)

    def long_body(t, carry):
        run_blocks(odd + 2 * (pairs % 2) + 4 * (quads % 2) + ATTN_UNROLL * t, ATTN_UNROLL)
        return carry

    lax.fori_loop(0, quads // 2, long_body, 0)

    outs = [acc_ref[h * HEAD_DIM:(h + 1) * HEAD_DIM, :] * (1.0 / l_ref[h:h + 1, :]) for h in range(N_HEADS)]
    out = jnp.concatenate(outs, axis=0).T
    y_attn = (out * sga_ref[...].astype(F32)).astype(BF16)

    y = y_out_conv + _dot(y_attn, wout_ref[D_CONV:, :])
    z = x_ref[0] + mod_ref[0, 2:3, :] * y
    r = lax.rsqrt(jnp.mean(z * z, axis=-1, keepdims=True) + EPS)
    o_ref[0] = z * r * gf_ref[...]


def _attention_outproj(qt, bias_t, k, vt, sga, x, mod3, y_conv, w_out_bf, g_final):
    bsz, seq, _ = x.shape
    nblk = seq // BLOCK
    assert nblk == SEL_LANES, "one selection lane per key block"
    row = lambda b, i: (b * nblk + i, 0)
    col = lambda b, i: (0, b * nblk + i)
    const = lambda b, i: (0, 0)
    xspec = pl.BlockSpec((1, BLOCK, D_MODEL), lambda b, i: (b, i, 0))
    lane = lax.broadcasted_iota(jnp.int32, (nblk, BLOCK, LANES), 2)
    blk = lax.broadcasted_iota(jnp.int32, (nblk, BLOCK, LANES), 0)
    onehot = ((lane % SEL_LANES) == blk).astype(BF16)
    return pl.pallas_call(
        _attn_body,
        grid=(bsz, nblk),
        in_specs=[pl.BlockSpec((D_ATTN, BLOCK), col),
                  pl.BlockSpec((LANES, BLOCK), col),
                  pl.BlockSpec((seq, D_ATTN), lambda b, i: (b, 0)),
                  pl.BlockSpec((N_HEADS * V_ROWS, seq), lambda b, i: (0, b)),
                  pl.BlockSpec((BLOCK, D_ATTN), row),
                  pl.BlockSpec((nblk, BLOCK, LANES), lambda b, i: (0, 0, 0)),
                  xspec,
                  pl.BlockSpec((1, 3, D_MODEL), lambda b, i: (b, 0, 0)),
                  pl.BlockSpec((BLOCK, D_CONV), row),
                  pl.BlockSpec((D_MODEL, D_MODEL), const),
                  pl.BlockSpec((1, D_MODEL), const)],
        out_specs=xspec,
        out_shape=jax.ShapeDtypeStruct(x.shape, F32),
        scratch_shapes=[pltpu.VMEM((N_HEADS, 2 * LANES, BLOCK), BF16),
                        pltpu.VMEM((2, N_HEADS, BLOCK, BLOCK), F32),
                        pltpu.VMEM((2, N_HEADS, BLOCK), F32),
                        pltpu.VMEM((N_HEADS, BLOCK), F32),
                        pltpu.VMEM((N_HEADS, BLOCK), F32),
                        pltpu.VMEM((D_ATTN, BLOCK), F32)],
        compiler_params=pltpu.CompilerParams(dimension_semantics=("arbitrary", "arbitrary"),
                                             vmem_limit_bytes=VMEM_LIMIT),
        name="attention_outproj",
    )(qt, bias_t, k, vt, sga, onehot, x, mod3, y_conv, w_out_bf, g_final.reshape(1, -1))


def kernel(x, c, positions, w_ada, b_ada, g_norm, w_in, w_dw, b_dw, g_ln_conv, b_ln_conv, w_pw, b_pw,
           w_out, g_final):
    bsz, seq, _ = x.shape
    assert w_ada.shape[0] == 1, "single-layer block: the final rmsnorm is fused into the last kernel"
    cos_t, sin_t = _rope_tables(positions)
    mod3 = _adaln(c, w_ada[0], b_ada[0]).reshape(bsz, 3, D_MODEL)
    w_hi = w_in[0].astype(BF16)
    y_conv, qt, k, vt, sga, bias_t = _inproj(x, mod3, g_norm[0].reshape(1, -1), cos_t, sin_t, w_hi,
                                             w_dw[0], b_dw[0], g_ln_conv[0], b_ln_conv[0],
                                             w_pw[0].astype(BF16), b_pw[0])
    return _attention_outproj(qt, bias_t, k, vt, sga, x, mod3, y_conv, w_out[0].astype(BF16), g_final)
```

```python
import jax
import jax.numpy as jnp
from jax import lax
from jax.experimental import pallas as pl
from jax.experimental.pallas import tpu as pltpu

F32 = jnp.float32
BF16 = jnp.bfloat16

D_MODEL = 1024
D_CONV = 512
D_ATTN = 512
HEAD_DIM = 64
N_HEADS = 8
CONV_WIDTH = 31
BLOCK = 256
TOP_K = 3
ROPE_DIM = 16
ROPE_HALF = 8
ROPE_THETA = 500000.0
EPS = 1e-6
NEG = -1e30
Q_SCALE = HEAD_DIM ** -0.5 * 1.4426950408889634
D_IN = 3584
C_VAL, C_GLU, C_GATE, C_Q, C_K, C_V, C_AG = 0, 512, 1024, 1536, 2048, 2560, 3072

LANES = 128
SUBLANES = 8
SEL_LANES = LANES // N_HEADS
V_ROWS = HEAD_DIM + 16
HALO = 32
CONV_ROWS = BLOCK + HALO
CONV_CHUNK = 32
ATTN_UNROLL = 8
VMEM_LIMIT = 48 * 1024 * 1024


def _split_bf16(a):
    hi = a.astype(BF16)
    lo = (a - hi.astype(F32)).astype(BF16)
    return hi, lo


def _dot(a, b):
    return jnp.dot(a, b, preferred_element_type=F32)


def _dot3(a_hi, a_lo, b_hi, b_lo):
    return _dot(a_hi, b_hi) + (_dot(a_hi, b_lo) + _dot(a_lo, b_hi))


def _sigmoid(a):
    return 1.0 / (1.0 + jnp.exp(-a))


def _silu(a):
    return a * _sigmoid(a)


def _adaln_body(c_ref, w_ref, b_ref, o_ref):
    a_hi, a_lo = _split_bf16(_silu(c_ref[...]))
    w_hi, w_lo = _split_bf16(w_ref[...])
    o_ref[...] = _dot3(a_hi, a_lo, w_hi, w_lo) + b_ref[...]


def _adaln(c, w_ada, b_ada):
    bsz = c.shape[0]
    n_out = w_ada.shape[1]
    tn = 1024
    return pl.pallas_call(
        _adaln_body,
        grid=(n_out // tn,),
        in_specs=[pl.BlockSpec((bsz, D_MODEL), lambda j: (0, 0)),
                  pl.BlockSpec((D_MODEL, tn), lambda j: (0, j)),
                  pl.BlockSpec((1, tn), lambda j: (0, j))],
        out_specs=pl.BlockSpec((bsz, tn), lambda j: (0, j)),
        out_shape=jax.ShapeDtypeStruct((bsz, n_out), F32),
        compiler_params=pltpu.CompilerParams(dimension_semantics=("arbitrary",),
                                             vmem_limit_bytes=VMEM_LIMIT),
        name="adaln",
    )(c, w_ada, b_ada.reshape(1, n_out))


def _rope_table_body(inv_ref, pos_ref, cos_ref, sin_ref):
    pos = pos_ref[...].astype(F32)
    for i in range(ROPE_HALF):
        ang = pos * inv_ref[i]
        cos_ref[i] = jnp.cos(ang)
        sin_ref[i] = jnp.sin(ang)


def _rope_tables(positions):
    bsz, seq = positions.shape
    inv = ROPE_THETA ** (-(jnp.arange(ROPE_HALF, dtype=F32) * 2.0) / ROPE_DIM)
    cos, sin = pl.pallas_call(
        _rope_table_body,
        in_specs=[pl.BlockSpec(memory_space=pltpu.SMEM),
                  pl.BlockSpec((bsz, seq), lambda: (0, 0))],
        out_specs=[pl.BlockSpec((ROPE_HALF, bsz, seq), lambda: (0, 0, 0))] * 2,
        out_shape=[jax.ShapeDtypeStruct((ROPE_HALF, bsz, seq), F32)] * 2,
        name="rope_table",
    )(inv, positions)
    return cos.reshape(ROPE_HALF, bsz * seq), sin.reshape(ROPE_HALF, bsz * seq)


def _rope_t(t, cos, sin):
    pieces = []
    for hd in range(N_HEADS):
        r0 = hd * HEAD_DIM
        t1 = t[r0:r0 + ROPE_HALF]
        t2 = t[r0 + ROPE_HALF:r0 + ROPE_DIM]
        pieces += [t1 * cos - t2 * sin, t2 * cos + t1 * sin, t[r0 + ROPE_DIM:r0 + HEAD_DIM]]
    return jnp.concatenate(pieces, axis=0)


def _conv_steps(u, win_ref, conv_ref, wdw_ref):
    def setup():
        win_ref[0, HALO:, :] = u
        for s in range(1, SUBLANES):
            win_ref[s, 0:CONV_ROWS - SUBLANES, :] = win_ref[0, s:s + CONV_ROWS - SUBLANES, :]

    def chunk(c, after):
        r0 = c * CONV_CHUNK
        base = HALO - (CONV_WIDTH - 1)
        zero = jnp.minimum(jnp.maximum(after[0:CONV_CHUNK, :], 0.0), 0.0)
        acc = jnp.concatenate([zero] * (D_CONV // zero.shape[1]), axis=1)
        for w in range(CONV_WIDTH):
            a, s = divmod(base + w, SUBLANES)
            rows = slice(r0 + a * SUBLANES, r0 + a * SUBLANES + CONV_CHUNK)
            tap = jnp.concatenate([wdw_ref[w]] * (CONV_CHUNK // SUBLANES), axis=0)
            acc = acc + win_ref[s, rows, :] * tap
        conv_ref[r0:r0 + CONV_CHUNK, :] = acc

    return [lambda after: setup()] + [lambda after, c=c: chunk(c, after) for c in range(BLOCK // CONV_CHUNK)]


def _conv_finish(gate_act, win_ref, conv_ref, bdw_ref, gln_ref, bln_ref, wpw_ref, bpw_ref):
    win_ref[0, 0:HALO, :] = win_ref[0, BLOCK:BLOCK + HALO, :]
    y = conv_ref[...] + bdw_ref[...]
    mu = jnp.mean(y, axis=-1, keepdims=True)
    yc = y - mu
    var = jnp.mean(yc * yc, axis=-1, keepdims=True)
    z = _silu(yc * lax.rsqrt(var + EPS) * gln_ref[...] + bln_ref[...])
    return (_dot(z.astype(BF16), wpw_ref[...]) + bpw_ref[...]) * gate_act


def _inproj_body(x_ref, mod_ref, g_ref, cos_ref, sin_ref, whi_ref,
                 wdw_ref, bdw_ref, gln_ref, bln_ref, wpw_ref, bpw_ref,
                 yc_ref, qt_ref, k_ref, vt_ref, sga_ref, bias_ref,
                 km_ref, win_ref, conv_ref):
    i = pl.program_id(1)

    @pl.when(i == 0)
    def _():
        km_ref[...] = jnp.zeros_like(km_ref)
        win_ref[0, 0:HALO, :] = jnp.zeros((HALO, D_CONV), F32)

    x = x_ref[0]
    r = lax.rsqrt(jnp.mean(x * x, axis=-1, keepdims=True) + EPS)
    shift = mod_ref[0, 0:1, :]
    scale = mod_ref[0, 1:2, :]
    h = (x * r) * (g_ref[...] * (1.0 + scale)) + shift
    h_hi = h.astype(BF16)

    def w_hi(c0):
        return whi_ref[:, c0:c0 + 512]

    u = _dot(h_hi, w_hi(C_VAL)) * _sigmoid(_dot(h_hi, w_hi(C_GLU)))
    plan = [(lambda: _dot(h_hi, w_hi(C_GATE)), 1), (lambda: _dot(h_hi, w_hi(C_V)), 2),
            (lambda: _dot(h_hi, w_hi(C_AG)), 2), (lambda: _dot(h_hi, w_hi(C_Q)), 2),
            (lambda: _dot(h_hi, w_hi(C_K)), 2)]
    conv_steps = _conv_steps(u, win_ref, conv_ref, wdw_ref)
    assert sum(n for _, n in plan) == len(conv_steps)
    products = []
    for matmul, n_steps in plan:
        products.append(matmul())
        for _ in range(n_steps):
            conv_steps.pop(0)(products[-1])
    c_gate, v, a_gate, q, k = products
    y_conv = _conv_finish(_silu(c_gate), win_ref, conv_ref, bdw_ref, gln_ref, bln_ref, wpw_ref, bpw_ref)
    yc_ref[...] = y_conv.astype(BF16)

    vt = v.T.astype(BF16)
    ones = jnp.ones((V_ROWS - HEAD_DIM, BLOCK), BF16)
    vt_ref[...] = jnp.concatenate(
        [piece for hd in range(N_HEADS) for piece in (vt[hd * HEAD_DIM:(hd + 1) * HEAD_DIM], ones)], axis=0)
    sga_ref[...] = _silu(a_gate).astype(BF16)

    cos = cos_ref[...]
    sin = sin_ref[...]
    qt = _rope_t(q.T, cos, sin) * Q_SCALE
    kr = _rope_t(k.T, cos, sin).T
    qt_hi = qt.astype(BF16)
    qt_ref[...] = qt_hi
    k_ref[...] = kr.astype(BF16)

    km = km_ref[...]
    lane_k = lax.broadcasted_iota(jnp.int32, km.shape, 1)
    kmt = jnp.concatenate([jnp.where((lane_k // HEAD_DIM) == hd, km, 0.0) for hd in range(N_HEADS)], axis=0)
    gt = _dot(kmt.astype(BF16), qt_hi)

    nb = km.shape[0]
    n_iota = lax.broadcasted_iota(jnp.int32, (nb, BLOCK), 0)
    past = n_iota < i
    n_sel = jnp.minimum(i, TOP_K)
    biases = []
    for hd in range(N_HEADS):
        g = jnp.where(past, gt[hd * nb:(hd + 1) * nb, :], NEG)
        rank = jnp.zeros((nb, BLOCK), jnp.int32)
        for n2 in range(nb):
            row = g[n2:n2 + 1, :]
            beats = (row > g) | ((row == g) & (n_iota > n2))
            rank = rank + jnp.where(beats, 1, 0)
        biases.append(jnp.where(rank < n_sel, 0.0, NEG))
    bias_ref[...] = jnp.concatenate(biases, axis=0).astype(BF16)

    km_ref[pl.ds(i, 1), :] = jnp.mean(kr, axis=0, keepdims=True)


def _inproj(x, mod3, g_norm, cos_t, sin_t, w_hi, w_dw, b_dw, g_ln, b_ln, w_pw_bf, b_pw):
    bsz, seq, _ = x.shape
    nblk = seq // BLOCK
    n = bsz * seq
    row = lambda b, i: (b * nblk + i, 0)
    col = lambda b, i: (0, b * nblk + i)
    const = lambda b, i: (0, 0)
    rows_out = pl.BlockSpec((BLOCK, 512), row)
    rows_sd = jax.ShapeDtypeStruct((n, 512), BF16)
    vec = pl.BlockSpec((1, D_CONV), const)
    return pl.pallas_call(
        _inproj_body,
        grid=(bsz, nblk),
        in_specs=[pl.BlockSpec((1, BLOCK, D_MODEL), lambda b, i: (b, i, 0)),
                  pl.BlockSpec((1, 3, D_MODEL), lambda b, i: (b, 0, 0)),
                  pl.BlockSpec((1, D_MODEL), const),
                  pl.BlockSpec((ROPE_HALF, BLOCK), col),
                  pl.BlockSpec((ROPE_HALF, BLOCK), col),
                  pl.BlockSpec((D_MODEL, D_IN), const),
                  pl.BlockSpec((CONV_WIDTH, SUBLANES, D_CONV), lambda b, i: (0, 0, 0)),
                  vec, vec, vec,
                  pl.BlockSpec((D_CONV, D_CONV), const),
                  vec],
        out_specs=[rows_out,
                   pl.BlockSpec((D_ATTN, BLOCK), col),
                   rows_out,
                   pl.BlockSpec((N_HEADS * V_ROWS, BLOCK), col),
                   rows_out,
                   pl.BlockSpec((LANES, BLOCK), col)],
        out_shape=[rows_sd,
                   jax.ShapeDtypeStruct((D_ATTN, n), BF16),
                   rows_sd,
                   jax.ShapeDtypeStruct((N_HEADS * V_ROWS, n), BF16),
                   rows_sd,
                   jax.ShapeDtypeStruct((LANES, n), BF16)],
        scratch_shapes=[pltpu.VMEM((nblk, D_ATTN), F32),
                        pltpu.VMEM((SUBLANES, CONV_ROWS, D_CONV), F32),
                        pltpu.VMEM((BLOCK, D_CONV), F32)],
        compiler_params=pltpu.CompilerParams(dimension_semantics=("arbitrary", "arbitrary"),
                                             vmem_limit_bytes=VMEM_LIMIT),
        name="inproj",
    )(x, mod3, g_norm, cos_t, sin_t, w_hi,
      jnp.broadcast_to(w_dw[:, None, :], (CONV_WIDTH, SUBLANES, D_CONV)),
      b_dw.reshape(1, -1), g_ln.reshape(1, -1), b_ln.reshape(1, -1), w_pw_bf, b_pw.reshape(1, -1))


def _attn_body(qt_ref, bias_ref, k_ref, vt_ref, sga_ref, onehot_ref, x_ref, mod_ref, yc_ref, wout_ref, gf_ref,
               o_ref, qa_ref, s_ref, smax_ref, m_ref, l_ref, acc_ref):
    i = pl.program_id(1)
    y_out_conv = _dot(yc_ref[...], wout_ref[0:D_CONV, :])
    zeros_q = jnp.zeros((HEAD_DIM, BLOCK), BF16)
    for h in range(N_HEADS):
        qh = qt_ref[h * HEAD_DIM:(h + 1) * HEAD_DIM, :]
        bh = bias_ref[h * SEL_LANES:(h + 1) * SEL_LANES, :]
        pieces = [qh, zeros_q] if h % 2 == 0 else [zeros_q, qh]
        if h > 0:
            pieces.append(jnp.zeros((h * SEL_LANES, BLOCK), BF16))
        pieces.append(bh)
        if h < N_HEADS - 1:
            pieces.append(jnp.zeros(((N_HEADS - 1 - h) * SEL_LANES, BLOCK), BF16))
        qa_ref[h] = jnp.concatenate(pieces, axis=0)

    def score_fn(j, own):
        start = pl.multiple_of(j * BLOCK, BLOCK)
        sel = jnp.zeros((BLOCK, LANES), BF16) if own else onehot_ref[j]
        operands = {}

        def f(h):
            pair = h // 2
            if pair not in operands:
                kj = k_ref[pl.ds(start, BLOCK), pair * LANES:(pair + 1) * LANES]
                operands[pair] = jnp.concatenate([kj, sel], axis=1)
            return _dot(operands[pair], qa_ref[h])
        return f

    def head_step(j, h, st, m_cur, first):
        start = pl.multiple_of(j * BLOCK, BLOCK)
        if first:
            m_new = m_cur
        else:
            m_prev = m_ref[h:h + 1, :]
            m_new = jnp.maximum(m_prev, m_cur)
            alpha = jnp.exp2(m_prev - m_new)
        pt = jnp.exp2(st - m_new)
        vth = vt_ref[h * V_ROWS:(h + 1) * V_ROWS, pl.ds(start, BLOCK)]
        pv_l = _dot(vth, pt.astype(BF16))
        pv = pv_l[0:HEAD_DIM]
        l_cur = pv_l[HEAD_DIM:HEAD_DIM + 1]
        rows = slice(h * HEAD_DIM, (h + 1) * HEAD_DIM)
        acc_ref[rows, :] = pv if first else alpha * acc_ref[rows, :] + pv
        l_ref[h:h + 1, :] = l_cur if first else alpha * l_ref[h:h + 1, :] + l_cur
        m_ref[h:h + 1, :] = m_new

    key_idx = lax.broadcasted_iota(jnp.int32, (BLOCK, BLOCK), 0)
    qry_idx = lax.broadcasted_iota(jnp.int32, (BLOCK, BLOCK), 1)
    causal = key_idx <= qry_idx
    own_fn = score_fn(i, True)
    own = [own_fn(h) for h in range(N_HEADS)]

    def overlapped(slot_next, j_next, j_cur, cur, first):
        nxt = score_fn(j_next, False)
        for h in range(N_HEADS):
            st_next = nxt(h)
            s_ref[slot_next, h] = st_next
            smax_ref[slot_next, h:h + 1, :] = jnp.max(st_next, axis=0, keepdims=True)
            head_step(j_cur, h, *cur(h), first)

    rem = i % ATTN_UNROLL
    odd = rem % 2
    def own_scores(h):
        st = jnp.where(causal, own[h], NEG)
        return st, jnp.max(st, axis=0, keepdims=True)

    def stored(slot):
        return lambda h: (s_ref[slot, h], smax_ref[slot, h:h + 1, :])

    overlapped(odd, 0, i, own_scores, True)

    def run_blocks(j0, count, first_slot):
        for d in range(count):
            slot = (first_slot + d) % 2
            overlapped(1 - slot, jnp.minimum(j0 + d + 1, i), j0 + d, stored(slot), False)

    for n_rem in range(1, ATTN_UNROLL):
        @pl.when(rem == n_rem)
        def _(n_rem=n_rem):
            run_blocks(0, n_rem, n_rem % 2)

    def long_body(t, carry):
        run_blocks(rem + ATTN_UNROLL * t, ATTN_UNROLL, 0)
        return carry

    lax.fori_loop(0, i // ATTN_UNROLL, long_body, 0)

    outs = [acc_ref[h * HEAD_DIM:(h + 1) * HEAD_DIM, :] * (1.0 / l_ref[h:h + 1, :]) for h in range(N_HEADS)]
    out = jnp.concatenate(outs, axis=0).T
    y_attn = (out * sga_ref[...].astype(F32)).astype(BF16)

    y = y_out_conv + _dot(y_attn, wout_ref[D_CONV:, :])
    z = x_ref[0] + mod_ref[0, 2:3, :] * y
    r = lax.rsqrt(jnp.mean(z * z, axis=-1, keepdims=True) + EPS)
    o_ref[0] = z * r * gf_ref[...]


def _attention_outproj(qt, bias_t, k, vt, sga, x, mod3, y_conv, w_out_bf, g_final):
    bsz, seq, _ = x.shape
    nblk = seq // BLOCK
    assert nblk == SEL_LANES, "one selection lane per key block"
    row = lambda b, i: (b * nblk + i, 0)
    col = lambda b, i: (0, b * nblk + i)
    const = lambda b, i: (0, 0)
    xspec = pl.BlockSpec((1, BLOCK, D_MODEL), lambda b, i: (b, i, 0))
    lane = lax.broadcasted_iota(jnp.int32, (nblk, BLOCK, LANES), 2)
    blk = lax.broadcasted_iota(jnp.int32, (nblk, BLOCK, LANES), 0)
    onehot = ((lane % SEL_LANES) == blk).astype(BF16)
    return pl.pallas_call(
        _attn_body,
        grid=(bsz, nblk),
        in_specs=[pl.BlockSpec((D_ATTN, BLOCK), col),
                  pl.BlockSpec((LANES, BLOCK), col),
                  pl.BlockSpec((seq, D_ATTN), lambda b, i: (b, 0)),
                  pl.BlockSpec((N_HEADS * V_ROWS, seq), lambda b, i: (0, b)),
                  pl.BlockSpec((BLOCK, D_ATTN), row),
                  pl.BlockSpec((nblk, BLOCK, LANES), lambda b, i: (0, 0, 0)),
                  xspec,
                  pl.BlockSpec((1, 3, D_MODEL), lambda b, i: (b, 0, 0)),
                  pl.BlockSpec((BLOCK, D_CONV), row),
                  pl.BlockSpec((D_MODEL, D_MODEL), const),
                  pl.BlockSpec((1, D_MODEL), const)],
        out_specs=xspec,
        out_shape=jax.ShapeDtypeStruct(x.shape, F32),
        scratch_shapes=[pltpu.VMEM((N_HEADS, 2 * LANES, BLOCK), BF16),
                        pltpu.VMEM((2, N_HEADS, BLOCK, BLOCK), F32),
                        pltpu.VMEM((2, N_HEADS, BLOCK), F32),
                        pltpu.VMEM((N_HEADS, BLOCK), F32),
                        pltpu.VMEM((N_HEADS, BLOCK), F32),
                        pltpu.VMEM((D_ATTN, BLOCK), F32)],
        compiler_params=pltpu.CompilerParams(dimension_semantics=("arbitrary", "arbitrary"),
                                             vmem_limit_bytes=VMEM_LIMIT),
        name="attention_outproj",
    )(qt, bias_t, k, vt, sga, onehot, x, mod3, y_conv, w_out_bf, g_final.reshape(1, -1))


def kernel(x, c, positions, w_ada, b_ada, g_norm, w_in, w_dw, b_dw, g_ln_conv, b_ln_conv, w_pw, b_pw,
           w_out, g_final):
    bsz, seq, _ = x.shape
    assert w_ada.shape[0] == 1, "single-layer block: the final rmsnorm is fused into the last kernel"
    cos_t, sin_t = _rope_tables(positions)
    mod3 = _adaln(c, w_ada[0], b_ada[0]).reshape(bsz, 3, D_MODEL)
    w_hi = w_in[0].astype(BF16)
    y_conv, qt, k, vt, sga, bias_t = _inproj(x, mod3, g_norm[0].reshape(1, -1), cos_t, sin_t, w_hi,
                                             w_dw[0], b_dw[0], g_ln_conv[0], b_ln_conv[0],
                                             w_pw[0].astype(BF16), b_pw[0])
    return _attention_outproj(qt, bias_t, k, vt, sga, x, mod3, y_conv, w_out[0].astype(BF16), g_final)
```

```python
import jax
import jax.numpy as jnp
from jax import lax
from jax.experimental import pallas as pl
from jax.experimental.pallas import tpu as pltpu

F32 = jnp.float32
BF16 = jnp.bfloat16

D_MODEL = 1024
D_CONV = 512
D_ATTN = 512
HEAD_DIM = 64
N_HEADS = 8
CONV_WIDTH = 31
BLOCK = 256
TOP_K = 3
ROPE_DIM = 16
ROPE_HALF = 8
ROPE_THETA = 500000.0
EPS = 1e-6
NEG = -1e30
Q_SCALE = HEAD_DIM ** -0.5 * 1.4426950408889634
D_IN = 3584
C_VAL, C_GLU, C_GATE, C_Q, C_K, C_V, C_AG = 0, 512, 1024, 1536, 2048, 2560, 3072

LANES = 128
SUBLANES = 8
SEL_LANES = LANES // N_HEADS
V_ROWS = HEAD_DIM + 16
HALO = 32
CONV_ROWS = BLOCK + HALO
CONV_CHUNK = 32
ATTN_UNROLL = 8
VMEM_LIMIT = 48 * 1024 * 1024


def _split_bf16(a):
    hi = a.astype(BF16)
    lo = (a - hi.astype(F32)).astype(BF16)
    return hi, lo


def _dot(a, b):
    return jnp.dot(a, b, preferred_element_type=F32)


def _dot3(a_hi, a_lo, b_hi, b_lo):
    return _dot(a_hi, b_hi) + (_dot(a_hi, b_lo) + _dot(a_lo, b_hi))


def _sigmoid(a):
    return 1.0 / (1.0 + jnp.exp(-a))


def _silu(a):
    return a * _sigmoid(a)


def _adaln_body(c_ref, w_ref, b_ref, o_ref):
    a_hi, a_lo = _split_bf16(_silu(c_ref[...]))
    w_hi, w_lo = _split_bf16(w_ref[...])
    o_ref[...] = _dot3(a_hi, a_lo, w_hi, w_lo) + b_ref[...]


def _adaln(c, w_ada, b_ada):
    bsz = c.shape[0]
    n_out = w_ada.shape[1]
    tn = 1024
    return pl.pallas_call(
        _adaln_body,
        grid=(n_out // tn,),
        in_specs=[pl.BlockSpec((bsz, D_MODEL), lambda j: (0, 0)),
                  pl.BlockSpec((D_MODEL, tn), lambda j: (0, j)),
                  pl.BlockSpec((1, tn), lambda j: (0, j))],
        out_specs=pl.BlockSpec((bsz, tn), lambda j: (0, j)),
        out_shape=jax.ShapeDtypeStruct((bsz, n_out), F32),
        compiler_params=pltpu.CompilerParams(dimension_semantics=("arbitrary",),
                                             vmem_limit_bytes=VMEM_LIMIT),
        name="adaln",
    )(c, w_ada, b_ada.reshape(1, n_out))


def _rope_table_body(inv_ref, pos_ref, cos_ref, sin_ref):
    pos = pos_ref[...].astype(F32)
    for i in range(ROPE_HALF):
        ang = pos * inv_ref[i]
        cos_ref[i] = jnp.cos(ang)
        sin_ref[i] = jnp.sin(ang)


def _rope_tables(positions):
    bsz, seq = positions.shape
    inv = ROPE_THETA ** (-(jnp.arange(ROPE_HALF, dtype=F32) * 2.0) / ROPE_DIM)
    cos, sin = pl.pallas_call(
        _rope_table_body,
        in_specs=[pl.BlockSpec(memory_space=pltpu.SMEM),
                  pl.BlockSpec((bsz, seq), lambda: (0, 0))],
        out_specs=[pl.BlockSpec((ROPE_HALF, bsz, seq), lambda: (0, 0, 0))] * 2,
        out_shape=[jax.ShapeDtypeStruct((ROPE_HALF, bsz, seq), F32)] * 2,
        name="rope_table",
    )(inv, positions)
    return cos.reshape(ROPE_HALF, bsz * seq), sin.reshape(ROPE_HALF, bsz * seq)


def _rope_t(t, cos, sin):
    pieces = []
    for hd in range(N_HEADS):
        r0 = hd * HEAD_DIM
        t1 = t[r0:r0 + ROPE_HALF]
        t2 = t[r0 + ROPE_HALF:r0 + ROPE_DIM]
        pieces += [t1 * cos - t2 * sin, t2 * cos + t1 * sin, t[r0 + ROPE_DIM:r0 + HEAD_DIM]]
    return jnp.concatenate(pieces, axis=0)


def _conv_steps(u, win_ref, conv_ref, wdw_ref):
    def setup():
        win_ref[0, HALO:, :] = u
        for s in range(1, SUBLANES):
            win_ref[s, 0:CONV_ROWS - SUBLANES, :] = win_ref[0, s:s + CONV_ROWS - SUBLANES, :]

    def chunk(c, after):
        r0 = c * CONV_CHUNK
        base = HALO - (CONV_WIDTH - 1)
        zero = jnp.minimum(jnp.maximum(after[0:CONV_CHUNK, :], 0.0), 0.0)
        acc = jnp.concatenate([zero] * (D_CONV // zero.shape[1]), axis=1)
        for w in range(CONV_WIDTH):
            a, s = divmod(base + w, SUBLANES)
            rows = slice(r0 + a * SUBLANES, r0 + a * SUBLANES + CONV_CHUNK)
            tap = jnp.concatenate([wdw_ref[w]] * (CONV_CHUNK // SUBLANES), axis=0)
            acc = acc + win_ref[s, rows, :] * tap
        conv_ref[r0:r0 + CONV_CHUNK, :] = acc

    return [lambda after: setup()] + [lambda after, c=c: chunk(c, after) for c in range(BLOCK // CONV_CHUNK)]


def _conv_finish(gate_act, win_ref, conv_ref, bdw_ref, gln_ref, bln_ref, wpw_ref, bpw_ref):
    win_ref[0, 0:HALO, :] = win_ref[0, BLOCK:BLOCK + HALO, :]
    y = conv_ref[...] + bdw_ref[...]
    mu = jnp.mean(y, axis=-1, keepdims=True)
    yc = y - mu
    var = jnp.mean(yc * yc, axis=-1, keepdims=True)
    z = _silu(yc * lax.rsqrt(var + EPS) * gln_ref[...] + bln_ref[...])
    return (_dot(z.astype(BF16), wpw_ref[...]) + bpw_ref[...]) * gate_act


def _inproj_body(x_ref, mod_ref, g_ref, cos_ref, sin_ref, whi_ref,
                 wdw_ref, bdw_ref, gln_ref, bln_ref, wpw_ref, bpw_ref,
                 yc_ref, qt_ref, k_ref, vt_ref, sga_ref, bias_ref,
                 km_ref, win_ref, conv_ref):
    i = pl.program_id(1)

    @pl.when(i == 0)
    def _():
        km_ref[...] = jnp.zeros_like(km_ref)
        win_ref[0, 0:HALO, :] = jnp.zeros((HALO, D_CONV), F32)

    x = x_ref[0]
    r = lax.rsqrt(jnp.mean(x * x, axis=-1, keepdims=True) + EPS)
    shift = mod_ref[0, 0:1, :]
    scale = mod_ref[0, 1:2, :]
    h = (x * r) * (g_ref[...] * (1.0 + scale)) + shift
    h_hi = h.astype(BF16)

    def w_hi(c0):
        return whi_ref[:, c0:c0 + 512]

    u = _dot(h_hi, w_hi(C_VAL)) * _sigmoid(_dot(h_hi, w_hi(C_GLU)))
    plan = [(lambda: _dot(h_hi, w_hi(C_GATE)), 1), (lambda: _dot(h_hi, w_hi(C_V)), 2),
            (lambda: _dot(h_hi, w_hi(C_AG)), 2), (lambda: _dot(h_hi, w_hi(C_Q)), 2),
            (lambda: _dot(h_hi, w_hi(C_K)), 2)]
    conv_steps = _conv_steps(u, win_ref, conv_ref, wdw_ref)
    assert sum(n for _, n in plan) == len(conv_steps)
    products = []
    for matmul, n_steps in plan:
        products.append(matmul())
        for _ in range(n_steps):
            conv_steps.pop(0)(products[-1])
    c_gate, v, a_gate, q, k = products
    y_conv = _conv_finish(_silu(c_gate), win_ref, conv_ref, bdw_ref, gln_ref, bln_ref, wpw_ref, bpw_ref)
    yc_ref[...] = y_conv.astype(BF16)

    vt = v.T.astype(BF16)
    ones = jnp.ones((V_ROWS - HEAD_DIM, BLOCK), BF16)
    vt_ref[...] = jnp.concatenate(
        [piece for hd in range(N_HEADS) for piece in (vt[hd * HEAD_DIM:(hd + 1) * HEAD_DIM], ones)], axis=0)
    sga_ref[...] = _silu(a_gate).astype(BF16)

    cos = cos_ref[...]
    sin = sin_ref[...]
    qt = _rope_t(q.T, cos, sin) * Q_SCALE
    kr = _rope_t(k.T, cos, sin).T
    qt_hi = qt.astype(BF16)
    qt_ref[...] = qt_hi
    k_ref[...] = kr.astype(BF16)

    km = km_ref[...]
    lane_k = lax.broadcasted_iota(jnp.int32, km.shape, 1)
    kmt = jnp.concatenate([jnp.where((lane_k // HEAD_DIM) == hd, km, 0.0) for hd in range(N_HEADS)], axis=0)
    gt = _dot(kmt.astype(BF16), qt_hi)

    nb = km.shape[0]
    n_iota = lax.broadcasted_iota(jnp.int32, (nb, BLOCK), 0)
    past = n_iota < i
    n_sel = jnp.minimum(i, TOP_K)
    biases = []
    for hd in range(N_HEADS):
        g = jnp.where(past, gt[hd * nb:(hd + 1) * nb, :], NEG)
        rank = jnp.zeros((nb, BLOCK), jnp.int32)
        for n2 in range(nb):
            row = g[n2:n2 + 1, :]
            beats = (row > g) | ((row == g) & (n_iota > n2))
            rank = rank + jnp.where(beats, 1, 0)
        biases.append(jnp.where(rank < n_sel, 0.0, NEG))
    bias_ref[...] = jnp.concatenate(biases, axis=0).astype(BF16)

    km_ref[pl.ds(i, 1), :] = jnp.mean(kr, axis=0, keepdims=True)


def _inproj(x, mod3, g_norm, cos_t, sin_t, w_hi, w_dw, b_dw, g_ln, b_ln, w_pw_bf, b_pw):
    bsz, seq, _ = x.shape
    nblk = seq // BLOCK
    n = bsz * seq
    row = lambda b, i: (b * nblk + i, 0)
    col = lambda b, i: (0, b * nblk + i)
    const = lambda b, i: (0, 0)
    rows_out = pl.BlockSpec((BLOCK, 512), row)
    rows_sd = jax.ShapeDtypeStruct((n, 512), BF16)
    vec = pl.BlockSpec((1, D_CONV), const)
    return pl.pallas_call(
        _inproj_body,
        grid=(bsz, nblk),
        in_specs=[pl.BlockSpec((1, BLOCK, D_MODEL), lambda b, i: (b, i, 0)),
                  pl.BlockSpec((1, 3, D_MODEL), lambda b, i: (b, 0, 0)),
                  pl.BlockSpec((1, D_MODEL), const),
                  pl.BlockSpec((ROPE_HALF, BLOCK), col),
                  pl.BlockSpec((ROPE_HALF, BLOCK), col),
                  pl.BlockSpec((D_MODEL, D_IN), const),
                  pl.BlockSpec((CONV_WIDTH, SUBLANES, D_CONV), lambda b, i: (0, 0, 0)),
                  vec, vec, vec,
                  pl.BlockSpec((D_CONV, D_CONV), const),
                  vec],
        out_specs=[rows_out,
                   pl.BlockSpec((D_ATTN, BLOCK), col),
                   rows_out,
                   pl.BlockSpec((N_HEADS * V_ROWS, BLOCK), col),
                   rows_out,
                   pl.BlockSpec((LANES, BLOCK), col)],
        out_shape=[rows_sd,
                   jax.ShapeDtypeStruct((D_ATTN, n), BF16),
                   rows_sd,
                   jax.ShapeDtypeStruct((N_HEADS * V_ROWS, n), BF16),
                   rows_sd,
                   jax.ShapeDtypeStruct((LANES, n), BF16)],
        scratch_shapes=[pltpu.VMEM((nblk, D_ATTN), F32),
                        pltpu.VMEM((SUBLANES, CONV_ROWS, D_CONV), F32),
                        pltpu.VMEM((BLOCK, D_CONV), F32)],
        compiler_params=pltpu.CompilerParams(dimension_semantics=("arbitrary", "arbitrary"),
                                             vmem_limit_bytes=VMEM_LIMIT),
        name="inproj",
    )(x, mod3, g_norm, cos_t, sin_t, w_hi,
      jnp.broadcast_to(w_dw[:, None, :], (CONV_WIDTH, SUBLANES, D_CONV)),
      b_dw.reshape(1, -1), g_ln.reshape(1, -1), b_ln.reshape(1, -1), w_pw_bf, b_pw.reshape(1, -1))


def _attn_body(qt_ref, bias_ref, k_ref, vt_ref, sga_ref, onehot_ref, x_ref, mod_ref, yc_ref, wout_ref, gf_ref,
               o_ref, qa_ref, s_ref, smax_ref, m_ref, l_ref, acc_ref):
    i = pl.program_id(1)
    y_out_conv = _dot(yc_ref[...], wout_ref[0:D_CONV, :])

    def build_query_operands():
        zeros_q = jnp.zeros((HEAD_DIM, BLOCK), BF16)
        for h in range(N_HEADS):
            qh = qt_ref[h * HEAD_DIM:(h + 1) * HEAD_DIM, :]
            bh = bias_ref[h * SEL_LANES:(h + 1) * SEL_LANES, :]
            pieces = [qh, zeros_q] if h % 2 == 0 else [zeros_q, qh]
            if h > 0:
                pieces.append(jnp.zeros((h * SEL_LANES, BLOCK), BF16))
            pieces.append(bh)
            if h < N_HEADS - 1:
                pieces.append(jnp.zeros(((N_HEADS - 1 - h) * SEL_LANES, BLOCK), BF16))
            qa_ref[h] = jnp.concatenate(pieces, axis=0)

    def score_fn(j, own):
        start = pl.multiple_of(j * BLOCK, BLOCK)
        sel = jnp.zeros((BLOCK, LANES), BF16) if own else onehot_ref[j]
        operands = {}

        def f(h):
            pair = h // 2
            if pair not in operands:
                kj = k_ref[pl.ds(start, BLOCK), pair * LANES:(pair + 1) * LANES]
                operands[pair] = jnp.concatenate([kj, sel], axis=1)
            return _dot(operands[pair], qa_ref[h])
        return f

    def head_step(j, h, st, m_cur, first):
        start = pl.multiple_of(j * BLOCK, BLOCK)
        if first:
            m_new = m_cur
        else:
            m_prev = m_ref[h:h + 1, :]
            m_new = jnp.maximum(m_prev, m_cur)
            alpha = jnp.exp2(m_prev - m_new)
        pt = jnp.exp2(st - m_new)
        vth = vt_ref[h * V_ROWS:(h + 1) * V_ROWS, pl.ds(start, BLOCK)]
        pv_l = _dot(vth, pt.astype(BF16))
        pv = pv_l[0:HEAD_DIM]
        l_cur = pv_l[HEAD_DIM:HEAD_DIM + 1]
        rows = slice(h * HEAD_DIM, (h + 1) * HEAD_DIM)
        acc_ref[rows, :] = pv if first else alpha * acc_ref[rows, :] + pv
        l_ref[h:h + 1, :] = l_cur if first else alpha * l_ref[h:h + 1, :] + l_cur
        m_ref[h:h + 1, :] = m_new

    def overlapped(slot_next, j_next, j_cur, cur, first):
        nxt = score_fn(j_next, False)
        for h in range(N_HEADS):
            st_next = nxt(h)
            s_ref[slot_next, h] = st_next
            smax_ref[slot_next, h:h + 1, :] = jnp.max(st_next, axis=0, keepdims=True)
            head_step(j_cur, h, *cur(h), first)

    def stored(slot):
        return lambda h: (s_ref[slot, h], smax_ref[slot, h:h + 1, :])

    def run_blocks(j0, count, first_slot):
        for d in range(count):
            slot = (first_slot + d) % 2
            overlapped(1 - slot, jnp.minimum(j0 + d + 1, i), j0 + d, stored(slot), False)

    def first_region(n_rem):
        build_query_operands()
        key_idx = lax.broadcasted_iota(jnp.int32, (BLOCK, BLOCK), 0)
        qry_idx = lax.broadcasted_iota(jnp.int32, (BLOCK, BLOCK), 1)
        causal = key_idx <= qry_idx
        own_fn = score_fn(i, True)
        own = [own_fn(h) for h in range(N_HEADS)]

        def own_scores(h):
            st = jnp.where(causal, own[h], NEG)
            return st, jnp.max(st, axis=0, keepdims=True)

        overlapped(n_rem % 2, 0, i, own_scores, True)
        run_blocks(0, n_rem, n_rem % 2)

    rem = i % ATTN_UNROLL
    for n_rem in range(ATTN_UNROLL):
        @pl.when(rem == n_rem)
        def _(n_rem=n_rem):
            first_region(n_rem)

    def long_body(t, carry):
        run_blocks(rem + ATTN_UNROLL * t, ATTN_UNROLL, 0)
        return carry

    lax.fori_loop(0, i // ATTN_UNROLL, long_body, 0)

    outs = [acc_ref[h * HEAD_DIM:(h + 1) * HEAD_DIM, :] * (1.0 / l_ref[h:h + 1, :]) for h in range(N_HEADS)]
    out = jnp.concatenate(outs, axis=0).T
    y_attn = (out * sga_ref[...].astype(F32)).astype(BF16)

    y = y_out_conv + _dot(y_attn, wout_ref[D_CONV:, :])
    z = x_ref[0] + mod_ref[0, 2:3, :] * y
    r = lax.rsqrt(jnp.mean(z * z, axis=-1, keepdims=True) + EPS)
    o_ref[0] = z * r * gf_ref[...]


def _attention_outproj(qt, bias_t, k, vt, sga, x, mod3, y_conv, w_out_bf, g_final):
    bsz, seq, _ = x.shape
    nblk = seq // BLOCK
    assert nblk == SEL_LANES, "one selection lane per key block"
    row = lambda b, i: (b * nblk + i, 0)
    col = lambda b, i: (0, b * nblk + i)
    const = lambda b, i: (0, 0)
    xspec = pl.BlockSpec((1, BLOCK, D_MODEL), lambda b, i: (b, i, 0))
    lane = lax.broadcasted_iota(jnp.int32, (nblk, BLOCK, LANES), 2)
    blk = lax.broadcasted_iota(jnp.int32, (nblk, BLOCK, LANES), 0)
    onehot = ((lane % SEL_LANES) == blk).astype(BF16)
    return pl.pallas_call(
        _attn_body,
        grid=(bsz, nblk),
        in_specs=[pl.BlockSpec((D_ATTN, BLOCK), col),
                  pl.BlockSpec((LANES, BLOCK), col),
                  pl.BlockSpec((seq, D_ATTN), lambda b, i: (b, 0)),
                  pl.BlockSpec((N_HEADS * V_ROWS, seq), lambda b, i: (0, b)),
                  pl.BlockSpec((BLOCK, D_ATTN), row),
                  pl.BlockSpec((nblk, BLOCK, LANES), lambda b, i: (0, 0, 0)),
                  xspec,
                  pl.BlockSpec((1, 3, D_MODEL), lambda b, i: (b, 0, 0)),
                  pl.BlockSpec((BLOCK, D_CONV), row),
                  pl.BlockSpec((D_MODEL, D_MODEL), const),
                  pl.BlockSpec((1, D_MODEL), const)],
        out_specs=xspec,
        out_shape=jax.ShapeDtypeStruct(x.shape, F32),
        scratch_shapes=[pltpu.VMEM((N_HEADS, 2 * LANES, BLOCK), BF16),
                        pltpu.VMEM((2, N_HEADS, BLOCK, BLOCK), F32),
                        pltpu.VMEM((2, N_HEADS, BLOCK), F32),
                        pltpu.VMEM((N_HEADS, BLOCK), F32),
                        pltpu.VMEM((N_HEADS, BLOCK), F32),
                        pltpu.VMEM((D_ATTN, BLOCK), F32)],
        compiler_params=pltpu.CompilerParams(dimension_semantics=("arbitrary", "arbitrary"),
                                             vmem_limit_bytes=VMEM_LIMIT),
        name="attention_outproj",
    )(qt, bias_t, k, vt, sga, onehot, x, mod3, y_conv, w_out_bf, g_final.reshape(1, -1))


def kernel(x, c, positions, w_ada, b_ada, g_norm, w_in, w_dw, b_dw, g_ln_conv, b_ln_conv, w_pw, b_pw,
           w_out, g_final):
    bsz, seq, _ = x.shape
    assert w_ada.shape[0] == 1, "single-layer block: the final rmsnorm is fused into the last kernel"
    cos_t, sin_t = _rope_tables(positions)
    mod3 = _adaln(c, w_ada[0], b_ada[0]).reshape(bsz, 3, D_MODEL)
    w_hi = w_in[0].astype(BF16)
    y_conv, qt, k, vt, sga, bias_t = _inproj(x, mod3, g_norm[0].reshape(1, -1), cos_t, sin_t, w_hi,
                                             w_dw[0], b_dw[0], g_ln_conv[0], b_ln_conv[0],
                                             w_pw[0].astype(BF16), b_pw[0])
    return _attention_outproj(qt, bias_t, k, vt, sga, x, mod3, y_conv, w_out[0].astype(BF16), g_final)
```
